```python
import jax, jax.numpy as jnp
from jax import lax
import numpy as np

D_MODEL = 1024
BATCH = 16
SEQ = 2048
DEPTH = 2
DEC_BATCH = 128
DEC_SEQ = 8
PAST_LEN = 16384
PAGE_SIZE = 128

HEAD_DIM = 64
SB_HEADS = 8
SB_KV_HEADS = 2
SB_GROUP = SB_HEADS // SB_KV_HEADS
SB_WIDTH = SB_HEADS * HEAD_DIM
SB_KV_WIDTH = SB_KV_HEADS * HEAD_DIM
MLA_HEADS = 8
NOPE_DIM = 64
ROPE_DIM = 32
V_DIM = 64
Q_RANK = 384
KV_RANK = 256
MLA_WIDTH = MLA_HEADS * V_DIM
MIX_WIDTH = SB_WIDTH + MLA_WIDTH
D_FF = 4 * D_MODEL
Q_BLOCK = 128
ROPE_THETA = 10000.0
EPS = 1e-6
SB_SCALE = HEAD_DIM ** -0.5
MLA_SCALE = (NOPE_DIM + ROPE_DIM) ** -0.5
IN_WIDTHS = (SB_WIDTH, SB_KV_WIDTH, SB_KV_WIDTH, Q_RANK, KV_RANK, ROPE_DIM)
IN_WIDTH = sum(IN_WIDTHS)
SPLIT_POINTS = tuple(int(v) for v in np.cumsum(IN_WIDTHS)[:-1])

kernel_name = "hybrid_stickbreaking_mla_step"


def rmsnorm(x, g):
    xf = x.astype(jnp.float32)
    y = xf * lax.rsqrt(jnp.mean(xf * xf, axis=-1, keepdims=True) + EPS)
    return (y * g.astype(jnp.float32)).astype(x.dtype)


def rope(x, pos):
    half = ROPE_DIM // 2
    inv = ROPE_THETA ** (-jnp.arange(half, dtype=jnp.float32) / half)
    ang = pos.astype(jnp.float32)[:, None] * inv[None, :]
    shp = (ang.shape[0],) + (1,) * (x.ndim - 3) + (half,)
    cos = jnp.cos(ang).reshape(shp)
    sin = jnp.sin(ang).reshape(shp)
    xf = x.astype(jnp.float32)
    x1, x2 = xf[..., :half], xf[..., half:]
    return jnp.concatenate([x1 * cos - x2 * sin, x2 * cos + x1 * sin], axis=-1).astype(x.dtype)


def modulation(c, w, b):
    m = jax.nn.silu(c) @ w + b
    return [t[:, None, :] for t in jnp.split(m, 6, axis=-1)]


def modulated_norm(x, g, shift, scale):
    return rmsnorm(x, g) * (1 + scale) + shift


def mlp(h, w1, w2):
    return jnp.square(jax.nn.relu(h @ w1)) @ w2


def project(h, pos, w_in, q_norm, w_uq, kv_norm, w_uk):
    B, T, _ = h.shape
    q_sb, k_sb, v_sb, c_q, c_kv, k_rope = jnp.split(h @ w_in, SPLIT_POINTS, axis=-1)
    q_sb = q_sb.reshape(B, T, SB_KV_HEADS, SB_GROUP, HEAD_DIM)
    k_sb = k_sb.reshape(B, T, SB_KV_HEADS, HEAD_DIM)
    v_sb = v_sb.reshape(B, T, SB_KV_HEADS, HEAD_DIM)
    q = jnp.einsum("btr,rhd->bthd", rmsnorm(c_q, q_norm), w_uq)
    q_rope = rope(q[..., NOPE_DIM:], pos)
    q_lat = jnp.einsum("bthd,chd->bthc", q[..., :NOPE_DIM], w_uk)
    ckv = rmsnorm(c_kv, kv_norm)
    krope = rope(k_rope, pos)
    return q_sb, k_sb, v_sb, q_lat, q_rope, ckv, krope


def sb_attend(q, k, v, q_pos, k_pos):
    B, T = q.shape[:2]
    z = jnp.einsum("btngd,bsnd->bngts", q, k, preferred_element_type=jnp.float32) * SB_SCALE
    mask = k_pos[None, :] < q_pos[:, None]
    log_keep = jnp.where(mask, jax.nn.log_sigmoid(-z), 0.0)
    after = lax.cumsum(log_keep, axis=log_keep.ndim - 1, reverse=True) - log_keep
    log_a = jnp.where(mask, z + log_keep + after, -jnp.inf)
    a = jnp.exp(log_a)
    o = jnp.einsum("bngts,bsnd->btngd", a.astype(v.dtype), v)
    return o.reshape(B, T, SB_WIDTH)


def mla_attend(q_lat, q_rope, ckv, krope, q_pos, k_pos):
    s = (jnp.einsum("bthc,bsc->bhts", q_lat, ckv, preferred_element_type=jnp.float32)
         + jnp.einsum("bthr,bsr->bhts", q_rope, krope, preferred_element_type=jnp.float32)) * MLA_SCALE
    s = jnp.where(k_pos[None, :] <= q_pos[:, None], s, -jnp.inf)
    p = jax.nn.softmax(s, axis=-1)
    return jnp.einsum("bhts,bsc->bthc", p.astype(ckv.dtype), ckv)


def merge(o_sb, o_lat, w_uv, g_osb, g_omla, w_out):
    B, T = o_sb.shape[:2]
    o_mla = jnp.einsum("bthc,chd->bthd", o_lat, w_uv).reshape(B, T, MLA_WIDTH)
    o = jnp.concatenate([rmsnorm(o_sb, g_osb), rmsnorm(o_mla, g_omla)], axis=-1)
    return o @ w_out


def gather_pages(pool, page_table):
    g = pool[page_table]
    return g.reshape((g.shape[0], g.shape[1] * g.shape[2]) + g.shape[3:])


def setup_inputs(seed: int = 0) -> dict:
    key = jax.random.key(seed)
    ks = iter(jax.random.split(key, 40))
    f32 = jnp.float32
    n_pages = PAST_LEN // PAGE_SIZE
    n_used = DEC_BATCH * n_pages
    n_pool = n_used + (n_used + 3) // 4

    def nrm(shape, scale=1.0):
        return jax.random.normal(next(ks), shape, f32) * scale

    def gain(shape):
        return 1.0 + nrm(shape, 0.02)

    page_table = jax.random.permutation(next(ks), n_pool)[:n_used].reshape(DEC_BATCH, n_pages).astype(jnp.int32)
    return {
        "x_prompt": nrm((BATCH, SEQ, D_MODEL)),
        "x_sample": nrm((DEC_BATCH, DEC_SEQ, D_MODEL)),
        "c_prompt": nrm((BATCH, D_MODEL)),
        "c_sample": nrm((DEC_BATCH, D_MODEL)),
        "cache_sb_k": nrm((DEPTH, n_pool, PAGE_SIZE, SB_KV_HEADS, HEAD_DIM)),
        "cache_sb_v": nrm((DEPTH, n_pool, PAGE_SIZE, SB_KV_HEADS, HEAD_DIM)),
        "cache_mla_ckv": nrm((DEPTH, n_pool, PAGE_SIZE, KV_RANK)),
        "cache_mla_krope": nrm((DEPTH, n_pool, PAGE_SIZE, ROPE_DIM)),
        "page_table": page_table,
        "w_ada": nrm((DEPTH, D_MODEL, 6 * D_MODEL), 0.5 * D_MODEL ** -0.5),
        "b_ada": nrm((DEPTH, 6 * D_MODEL), 0.02),
        "attn_norm": gain((DEPTH, D_MODEL)),
        "w_in": nrm((DEPTH, D_MODEL, IN_WIDTH), D_MODEL ** -0.5),
        "q_norm": gain((DEPTH, Q_RANK)),
        "w_uq": nrm((DEPTH, Q_RANK, MLA_HEADS, NOPE_DIM + ROPE_DIM), Q_RANK ** -0.5),
        "kv_norm": gain((DEPTH, KV_RANK)),
        "w_uk": nrm((DEPTH, KV_RANK, MLA_HEADS, NOPE_DIM), KV_RANK ** -0.5),
        "w_uv": nrm((DEPTH, KV_RANK, MLA_HEADS, V_DIM), KV_RANK ** -0.5),
        "out_norm_sb": gain((DEPTH, SB_WIDTH)),
        "out_norm_mla": gain((DEPTH, MLA_WIDTH)),
        "w_out": nrm((DEPTH, MIX_WIDTH, D_MODEL), MIX_WIDTH ** -0.5),
        "mlp_norm": gain((DEPTH, D_MODEL)),
        "w_mlp1": nrm((DEPTH, D_MODEL, D_FF), D_MODEL ** -0.5),
        "w_mlp2": nrm((DEPTH, D_FF, D_MODEL), D_FF ** -0.5),
        "final_norm": gain((D_MODEL,)),
    }


def reference(x_prompt, x_sample, c_prompt, c_sample, cache_sb_k, cache_sb_v, cache_mla_ckv,
              cache_mla_krope, page_table, w_ada, b_ada, attn_norm, w_in, q_norm, w_uq, kv_norm,
              w_uk, w_uv, out_norm_sb, out_norm_mla, w_out, mlp_norm, w_mlp1, w_mlp2, final_norm):
    seq = x_prompt.shape[1]
    dec_seq = x_sample.shape[1]
    n_past = page_table.shape[1] * PAGE_SIZE
    pos_p = jnp.arange(seq, dtype=jnp.int32)
    pos_s = n_past + jnp.arange(dec_seq, dtype=jnp.int32)
    k_pos_s = jnp.arange(n_past + dec_seq, dtype=jnp.int32)

    xp, xs = x_prompt, x_sample
    nk_p, nv_p, nc_p, nr_p = [], [], [], []
    nk_s, nv_s, nc_s, nr_s = [], [], [], []
    for l in range(DEPTH):
        proj_w = (w_in[l], q_norm[l], w_uq[l], kv_norm[l], w_uk[l])
        merge_w = (w_uv[l], out_norm_sb[l], out_norm_mla[l], w_out[l])

        sh1, sc1, g1, sh2, sc2, g2 = modulation(c_prompt, w_ada[l], b_ada[l])
        h = modulated_norm(xp, attn_norm[l], sh1, sc1)
        q_sb, k_sb, v_sb, q_lat, q_rope, ckv, krope = project(h, pos_p, *proj_w)
        o_sb_blocks, o_lat_blocks = [], []
        for start in range(0, seq, Q_BLOCK):
            end = min(start + Q_BLOCK, seq)
            qp, kp = pos_p[start:end], pos_p[:end]
            o_sb_blocks.append(sb_attend(q_sb[:, start:end], k_sb[:, :end], v_sb[:, :end], qp, kp))
            o_lat_blocks.append(mla_attend(q_lat[:, start:end], q_rope[:, start:end],
                                           ckv[:, :end], krope[:, :end], qp, kp))
        o_sb = jnp.concatenate(o_sb_blocks, axis=1)
        o_lat = jnp.concatenate(o_lat_blocks, axis=1)
        xp = xp + g1 * merge(o_sb, o_lat, *merge_w)
        xp = xp + g2 * mlp(modulated_norm(xp, mlp_norm[l], sh2, sc2), w_mlp1[l], w_mlp2[l])
        nk_p.append(k_sb); nv_p.append(v_sb); nc_p.append(ckv); nr_p.append(krope)

        sh1, sc1, g1, sh2, sc2, g2 = modulation(c_sample, w_ada[l], b_ada[l])
        h = modulated_norm(xs, attn_norm[l], sh1, sc1)
        q_sb, k_sb, v_sb, q_lat, q_rope, ckv, krope = project(h, pos_s, *proj_w)
        k_all = jnp.concatenate([gather_pages(cache_sb_k[l], page_table), k_sb], axis=1)
        v_all = jnp.concatenate([gather_pages(cache_sb_v[l], page_table), v_sb], axis=1)
        c_all = jnp.concatenate([gather_pages(cache_mla_ckv[l], page_table), ckv], axis=1)
        r_all = jnp.concatenate([gather_pages(cache_mla_krope[l], page_table), krope], axis=1)
        o_sb = sb_attend(q_sb, k_all, v_all, pos_s, k_pos_s)
        o_lat = mla_attend(q_lat, q_rope, c_all, r_all, pos_s, k_pos_s)
        xs = xs + g1 * merge(o_sb, o_lat, *merge_w)
        xs = xs + g2 * mlp(modulated_norm(xs, mlp_norm[l], sh2, sc2), w_mlp1[l], w_mlp2[l])
        nk_s.append(k_sb); nv_s.append(v_sb); nc_s.append(ckv); nr_s.append(krope)

    y_prompt = rmsnorm(xp, final_norm)
    y_sample = rmsnorm(xs, final_norm)
    return (y_prompt, y_sample,
            jnp.stack(nk_p), jnp.stack(nv_p), jnp.stack(nc_p), jnp.stack(nr_p),
            jnp.stack(nk_s), jnp.stack(nv_s), jnp.stack(nc_s), jnp.stack(nr_s))
```

```python
import functools

import jax
import jax.numpy as jnp
from jax import lax
from jax.experimental import pallas as pl
from jax.experimental.pallas import tpu as pltpu

HEAD_DIM = 64
SB_HEADS = 8
SB_KV_HEADS = 2
SB_GROUP = SB_HEADS // SB_KV_HEADS
MLA_HEADS = 8
NOPE_DIM = 64
ROPE_DIM = 32
V_DIM = 64
Q_RANK = 384
KV_RANK = 256
PAGE_SIZE = 128
ROPE_THETA = 10000.0
EPS = 1e-6
SB_SCALE = HEAD_DIM ** -0.5
MLA_SCALE = (NOPE_DIM + ROPE_DIM) ** -0.5

LANE = 128
SLOT = LANE
SB_Q_EXT = SB_HEADS * SLOT
MLA_Q_EXT = MLA_HEADS * SLOT
OFF_QSB = 0
OFF_KSB = OFF_QSB + SB_Q_EXT
OFF_VSB = OFF_KSB + SB_KV_HEADS * HEAD_DIM
OFF_CQ = OFF_VSB + SB_KV_HEADS * HEAD_DIM
OFF_CKV = OFF_CQ + Q_RANK
OFF_KR = OFF_CKV + KV_RANK
OFF_KRS = OFF_KR + SLOT
IN_EXT = OFF_KRS + SLOT

VMEM_LIMIT = 56 * 1024 * 1024
NEG_BIG = -1e30
TOKEN_TILE = 512
Q_TILE = 256
PAGES_PER_STEP = 8

BF16 = jnp.bfloat16
F32 = jnp.float32


def _rms(x, g):
    return x * lax.rsqrt(jnp.mean(x * x, axis=-1, keepdims=True) + EPS) * g


def _dot(a, b):
    return jnp.dot(a, b, preferred_element_type=F32)


def _dot_nt(a, b):
    return lax.dot_general(a, b, (((1,), (1,)), ((), ())), preferred_element_type=F32)


def _split_bf16(x):
    hi = x.astype(BF16)
    lo = (x - hi.astype(F32)).astype(BF16)
    return hi, lo


def _mod_body(c_ref, w_ref, b_ref, out_ref):
    c = c_ref[...]
    s = c * jax.nn.sigmoid(c)
    out_ref[0] = _dot(s.astype(BF16), w_ref[0]) + b_ref[0]


def _modulation(c_all, w_ada_bf, b_ada):
    depth, d, n6 = w_ada_bf.shape
    rows = c_all.shape[0]
    tn = 1536
    return pl.pallas_call(
        _mod_body,
        grid=(depth, n6 // tn),
        in_specs=[
            pl.BlockSpec((rows, d), lambda l, j: (0, 0)),
            pl.BlockSpec((1, d, tn), lambda l, j: (l, 0, j)),
            pl.BlockSpec((1, 1, tn), lambda l, j: (l, 0, j)),
        ],
        out_specs=pl.BlockSpec((1, rows, tn), lambda l, j: (l, 0, j)),
        out_shape=jax.ShapeDtypeStruct((depth, rows, n6), F32),
        compiler_params=pltpu.CompilerParams(
            dimension_semantics=("arbitrary", "arbitrary"), vmem_limit_bytes=VMEM_LIMIT),
        name="modulation",
    )(c_all, w_ada_bf, b_ada.reshape(depth, 1, n6))


def _proj_common(x_ref, sh_ref, sc_ref, g_ref, win_ref, qn_ref, wuq_ref, kvn_ref,
                 cq_ref, sq_ref, ck_ref, sk_ref, ksb_ref, vsb_ref, ckv_ref, kr_ref):
    x = x_ref[0]
    h = _rms(x, g_ref[...]) * (1.0 + sc_ref[0]) + sh_ref[0]
    proj = _dot(h.astype(BF16), win_ref[...])
    ksb = proj[:, OFF_KSB:OFF_VSB]
    vsb = proj[:, OFF_VSB:OFF_CQ]
    ksb_ref[0] = ksb
    vsb_ref[0] = vsb
    cqn = _rms(proj[:, OFF_CQ:OFF_CKV], qn_ref[...])
    q2 = _dot(cqn.astype(BF16), wuq_ref[...])
    cq_t = jnp.concatenate([cq_ref[...]] * MLA_HEADS, axis=1)
    sq_t = jnp.concatenate([sq_ref[...]] * MLA_HEADS, axis=1)
    qm = q2[:, :MLA_Q_EXT] * cq_t + q2[:, MLA_Q_EXT:] * sq_t
    ckv = _rms(proj[:, OFF_CKV:OFF_KR], kvn_ref[...])
    ckv_ref[0] = ckv
    krp = proj[:, OFF_KR:OFF_KRS] * ck_ref[...] + proj[:, OFF_KRS:IN_EXT] * sk_ref[...]
    kr_ref[0] = krp[:, NOPE_DIM:NOPE_DIM + ROPE_DIM]
    return proj[:, OFF_QSB:OFF_KSB], ksb, vsb, qm, ckv, krp


def _proj_prompt_body(x_ref, sh_ref, sc_ref, g_ref, win_ref, qn_ref, wuq_ref, kvn_ref,
                      cq_ref, sq_ref, ck_ref, sk_ref, wk_ref, wv_ref,
                      ksb_ref, vsb_ref, ckv_ref, kr_ref,
                      qsb_o, ksbb_o, vsbb_o, qm_o, km_o, vm_o):
    qsb, ksb, vsb, qm, ckv, krp = _proj_common(
        x_ref, sh_ref, sc_ref, g_ref, win_ref, qn_ref, wuq_ref, kvn_ref,
        cq_ref, sq_ref, ck_ref, sk_ref, ksb_ref, vsb_ref, ckv_ref, kr_ref)
    ksbb_o[0] = ksb.astype(BF16)
    vsbb_o[0] = vsb.astype(BF16)
    ckv_b = ckv.astype(BF16)
    kn = _dot(ckv_b, wk_ref[...])
    vm = _dot(ckv_b, wv_ref[...])
    for h in range(SB_HEADS):
        qsb_o[0, h] = qsb[:, h * SLOT:(h + 1) * SLOT].astype(BF16)
    for h in range(MLA_HEADS):
        qm_o[0, h] = qm[:, h * SLOT:(h + 1) * SLOT].astype(BF16)
        km_o[0, h] = (kn[:, h * SLOT:(h + 1) * SLOT] + krp).astype(BF16)
    for p in range(MLA_HEADS // 2):
        vm_o[0, p] = vm[:, p * SLOT:(p + 1) * SLOT].astype(BF16)


def _proj_sample_body(x_ref, sh_ref, sc_ref, g_ref, win_ref, qn_ref, wuq_ref, kvn_ref,
                      cq_ref, sq_ref, ck_ref, sk_ref, wuk_ref,
                      ksb_ref, vsb_ref, ckv_ref, kr_ref,
                      qsb_o, qm_o, qlat_o):
    qsb, _, _, qm, _, _ = _proj_common(
        x_ref, sh_ref, sc_ref, g_ref, win_ref, qn_ref, wuq_ref, kvn_ref,
        cq_ref, sq_ref, ck_ref, sk_ref, ksb_ref, vsb_ref, ckv_ref, kr_ref)
    qsb_o[0] = qsb.astype(BF16)
    qm_b = qm.astype(BF16)
    qm_o[0] = qm_b
    for h in range(MLA_HEADS):
        qlat_o[0, :, h * KV_RANK:(h + 1) * KV_RANK] = _dot(
            qm_b[:, h * SLOT:(h + 1) * SLOT], wuk_ref[h]).astype(BF16)


def _full(shape):
    n = len(shape)
    return pl.BlockSpec(shape, lambda b, i: (0,) * n)


def _proj(x, sh, sc, g, wl, tabs, tile, absorbed):
    bsz, t, d = x.shape
    r = sh.shape[1]
    nt = t // tile
    if r == 1:
        mod_spec = pl.BlockSpec((1, 1, d), lambda b, i: (b, 0, 0))
    else:
        mod_spec = pl.BlockSpec((1, tile, d), lambda b, i: (b, i, 0))
    tab_spec = pl.BlockSpec((tile, SLOT), lambda b, i: (i, 0))
    tok = lambda w: pl.BlockSpec((1, tile, w), lambda b, i: (b, i, 0))
    heads = lambda n, w: pl.BlockSpec((1, n, tile, w), lambda b, i: (b, 0, i, 0))
    common_in = [
        tok(d), mod_spec, mod_spec, _full((1, d)),
        _full((d, IN_EXT)), _full((1, Q_RANK)), _full((Q_RANK, 2 * MLA_Q_EXT)), _full((1, KV_RANK)),
        tab_spec, tab_spec, tab_spec, tab_spec,
    ]
    common_args = [x, sh, sc, g, wl["win"], wl["qn"], wl["wuq"], wl["kvn"], *tabs]
    common_out_specs = [tok(SB_KV_HEADS * HEAD_DIM), tok(SB_KV_HEADS * HEAD_DIM), tok(KV_RANK), tok(ROPE_DIM)]
    common_out_shape = [
        jax.ShapeDtypeStruct((bsz, t, SB_KV_HEADS * HEAD_DIM), F32),
        jax.ShapeDtypeStruct((bsz, t, SB_KV_HEADS * HEAD_DIM), F32),
        jax.ShapeDtypeStruct((bsz, t, KV_RANK), F32),
        jax.ShapeDtypeStruct((bsz, t, ROPE_DIM), F32),
    ]
    if not absorbed:
        body = _proj_prompt_body
        in_specs = common_in + [_full((KV_RANK, MLA_Q_EXT)), _full((KV_RANK, MLA_HEADS * V_DIM))]
        args = common_args + [wl["wk"], wl["wv"]]
        out_specs = common_out_specs + [
            heads(SB_HEADS, SLOT), tok(SLOT), tok(SLOT),
            heads(MLA_HEADS, SLOT), heads(MLA_HEADS, SLOT), heads(MLA_HEADS // 2, SLOT)]
        out_shape = common_out_shape + [
            jax.ShapeDtypeStruct((bsz, SB_HEADS, t, SLOT), BF16),
            jax.ShapeDtypeStruct((bsz, t, SLOT), BF16),
            jax.ShapeDtypeStruct((bsz, t, SLOT), BF16),
            jax.ShapeDtypeStruct((bsz, MLA_HEADS, t, SLOT), BF16),
            jax.ShapeDtypeStruct((bsz, MLA_HEADS, t, SLOT), BF16),
            jax.ShapeDtypeStruct((bsz, MLA_HEADS // 2, t, SLOT), BF16),
        ]
        name = "proj_prompt"
    else:
        body = _proj_sample_body
        in_specs = common_in + [pl.BlockSpec((MLA_HEADS, SLOT, KV_RANK), lambda b, i: (0, 0, 0))]
        args = common_args + [wl["wuk_abs"]]
        out_specs = common_out_specs + [tok(SB_Q_EXT), tok(MLA_Q_EXT), tok(MLA_HEADS * KV_RANK)]
        out_shape = common_out_shape + [
            jax.ShapeDtypeStruct((bsz, t, SB_Q_EXT), BF16),
            jax.ShapeDtypeStruct((bsz, t, MLA_Q_EXT), BF16),
            jax.ShapeDtypeStruct((bsz, t, MLA_HEADS * KV_RANK), BF16),
        ]
        name = "proj_sample"
    return pl.pallas_call(
        body,
        grid=(bsz, nt),
        in_specs=in_specs,
        out_specs=out_specs,
        out_shape=out_shape,
        compiler_params=pltpu.CompilerParams(
            dimension_semantics=("arbitrary", "arbitrary"), vmem_limit_bytes=VMEM_LIMIT),
        name=name,
    )(*args)


def _sb_block(q, k, v, upper, o, acc, mask):
    z = _dot_nt(q, k)
    log1pe = jnp.log1p(jnp.exp(-jnp.abs(z)))
    lk = -(jnp.maximum(z, 0.0) + log1pe)
    if mask is not None:
        lk = jnp.where(mask, lk, 0.0)
    hi, lo = _split_bf16(lk)
    after = _dot(hi, upper) + _dot(lo, upper)
    log_a = jnp.minimum(z, 0.0) - log1pe + after + acc
    a = jnp.exp(log_a)
    if mask is not None:
        a = jnp.where(mask, a, 0.0)
    o = o + _dot(a.astype(BF16), v)
    acc = acc + jnp.sum(lk, axis=1, keepdims=True)
    return o, acc


def _softmax_block(s, v, m, l, o):
    m_new = jnp.maximum(m, jnp.max(s, axis=1, keepdims=True))
    alpha = jnp.exp(m - m_new)
    p = jnp.exp(s - m_new)
    l = alpha * l + jnp.sum(p, axis=1, keepdims=True)
    o = alpha * o + _dot(p.astype(BF16), v)
    return m_new, l, o


def _upper(n):
    return (lax.broadcasted_iota(jnp.int32, (n, n), 0)
            > lax.broadcasted_iota(jnp.int32, (n, n), 1)).astype(BF16)


def _pattn_body(tq, qsb_ref, ksb_ref, vsb_ref, qm_ref, km_ref, vm_ref, osb_ref, om_ref, acc_scr):
    i = pl.program_id(1)
    row = lax.broadcasted_iota(jnp.int32, (tq, tq), 0)
    col = lax.broadcasted_iota(jnp.int32, (tq, tq), 1)
    strict = col < row
    incl = col <= row
    upper = _upper(tq)
    lane = lax.broadcasted_iota(jnp.int32, (tq, SLOT), 1)
    low_half = lane < HEAD_DIM

    def sb_head(h, carry):
        q = qsb_ref[0, h]

        def block(j, c, mask):
            ks = pl.multiple_of(j * tq, tq)
            k = ksb_ref[0, pl.ds(ks, tq), :]
            v = vsb_ref[0, pl.ds(ks, tq), :]
            return _sb_block(q, k, v, upper, c[0], c[1], mask)

        c = block(i, (jnp.zeros((tq, SLOT), F32), jnp.zeros((tq, 1), F32)), strict)
        c = lax.fori_loop(0, i, lambda jj, cc: block(i - 1 - jj, cc, None), c)
        acc_scr[h] = c[0]
        return carry

    lax.fori_loop(0, SB_HEADS, sb_head, 0)
    for c in range(SB_HEADS // 2):
        a, b = acc_scr[2 * c], acc_scr[2 * c + 1]
        if (2 * c) // SB_GROUP == 0:
            b = pltpu.roll(b, HEAD_DIM, axis=1)
        else:
            a = pltpu.roll(a, HEAD_DIM, axis=1)
        osb_ref[0, :, c * SLOT:(c + 1) * SLOT] = jnp.where(low_half, a, b)

    def mla_head(h, carry):
        q = qm_ref[0, h]
        hp = lax.shift_right_logical(h, 1)

        def scores(j):
            ks = pl.multiple_of(j * tq, tq)
            k = km_ref[0, h, pl.ds(ks, tq), :]
            v = vm_ref[0, hp, pl.ds(ks, tq), :]
            return _dot_nt(q, k), v

        s, v = scores(i)
        s = jnp.where(incl, s, NEG_BIG)
        m = jnp.max(s, axis=1, keepdims=True)
        p = jnp.exp(s - m)
        l = jnp.sum(p, axis=1, keepdims=True)
        o = _dot(p.astype(BF16), v)

        def body(j, c):
            s, v = scores(j)
            return _softmax_block(s, v, *c)

        m, l, o = lax.fori_loop(0, i, body, (m, l, o))
        acc_scr[h] = o / l
        return carry

    lax.fori_loop(0, MLA_HEADS, mla_head, 0)
    for c in range(MLA_HEADS // 2):
        om_ref[0, :, c * SLOT:(c + 1) * SLOT] = jnp.where(low_half, acc_scr[2 * c], acc_scr[2 * c + 1])


def _prompt_attention(qsb, ksb, vsb, qm, km, vm, tq):
    bsz, _, t, _ = qsb.shape
    nq = t // tq
    qspec = pl.BlockSpec((1, SB_HEADS, tq, SLOT), lambda b, i: (b, 0, i, 0))
    kv2 = pl.BlockSpec((1, t, SLOT), lambda b, i: (b, 0, 0))
    kfull = lambda n: pl.BlockSpec((1, n, t, SLOT), lambda b, i: (b, 0, 0, 0))
    ospec = pl.BlockSpec((1, tq, SB_HEADS * HEAD_DIM), lambda b, i: (b, i, 0))
    return pl.pallas_call(
        functools.partial(_pattn_body, tq),
        grid=(bsz, nq),
        in_specs=[qspec, kv2, kv2, qspec, kfull(MLA_HEADS), kfull(MLA_HEADS // 2)],
        out_specs=[ospec, ospec],
        out_shape=[jax.ShapeDtypeStruct((bsz, t, SB_HEADS * HEAD_DIM), F32),
                   jax.ShapeDtypeStruct((bsz, t, MLA_HEADS * V_DIM), F32)],
        scratch_shapes=[pltpu.VMEM((SB_HEADS, tq, SLOT), F32)],
        compiler_params=pltpu.CompilerParams(
            dimension_semantics=("arbitrary", "arbitrary"), vmem_limit_bytes=VMEM_LIMIT),
        name="prompt_attention",
    )(qsb, ksb, vsb, qm, km, vm)


def _dattn_body(pages_per_step, dec_seq, pt_ref, qsb_ref, ql_ref, qr_ref,
                ksn_ref, vsn_ref, cn_ref, rn_ref, wuv_ref, *rest):
    P = pages_per_step
    kp = rest[0:P]
    vp = rest[P:2 * P]
    cp = rest[2 * P:3 * P]
    rp = rest[3 * P:4 * P]
    osb_ref, om_ref, o_scr, acc_scr, m_scr, l_scr, ol_scr = rest[4 * P:]
    step = pl.program_id(1)
    rows = SB_HEADS * dec_seq
    upper = _upper(PAGE_SIZE)
    qsb = qsb_ref[0]
    ql = ql_ref[0]
    qr = qr_ref[0]

    @pl.when(step == 0)
    def _():
        pad = jnp.zeros((PAGE_SIZE - dec_seq, SLOT), F32)
        k = jnp.concatenate([ksn_ref[0], pad], axis=0).astype(BF16)
        v = jnp.concatenate([vsn_ref[0], pad], axis=0).astype(BF16)
        t_row = lax.rem(lax.broadcasted_iota(jnp.int32, (rows, PAGE_SIZE), 0), dec_seq)
        s_col = lax.broadcasted_iota(jnp.int32, (rows, PAGE_SIZE), 1)
        o, acc = _sb_block(qsb, k, v, upper, jnp.zeros((rows, SLOT), F32),
                           jnp.zeros((rows, 1), F32), s_col < t_row)
        o_scr[...] = o
        acc_scr[...] = acc
        c = jnp.concatenate([cn_ref[0], jnp.zeros((PAGE_SIZE - dec_seq, KV_RANK), F32)], axis=0).astype(BF16)
        r = jnp.concatenate([rn_ref[0], jnp.zeros((PAGE_SIZE - dec_seq, ROPE_DIM), F32)], axis=0).astype(BF16)
        s = _dot_nt(ql, c) + _dot_nt(qr, r)
        s = jnp.where(s_col <= t_row, s, NEG_BIG)
        m = jnp.max(s, axis=1, keepdims=True)
        p = jnp.exp(s - m)
        m_scr[...] = m
        l_scr[...] = jnp.sum(p, axis=1, keepdims=True)
        ol_scr[...] = _dot(p.astype(BF16), c)

    o, acc = o_scr[...], acc_scr[...]
    m, l, ol = m_scr[...], l_scr[...], ol_scr[...]
    for p in range(P - 1, -1, -1):
        k = kp[p][0, 0].astype(BF16)
        v = vp[p][0, 0].astype(BF16)
        o, acc = _sb_block(qsb, k, v, upper, o, acc, None)
        c = cp[p][0, 0].astype(BF16)
        r = rp[p][0, 0].astype(BF16)
        s = _dot_nt(ql, c) + _dot_nt(qr, r)
        m, l, ol = _softmax_block(s, c, m, l, ol)
    o_scr[...] = o
    acc_scr[...] = acc
    m_scr[...] = m
    l_scr[...] = l
    ol_scr[...] = ol

    @pl.when(step == pl.num_programs(1) - 1)
    def _():
        osb_ref[0] = o
        oln = (ol / l).astype(BF16)
        for h in range(MLA_HEADS):
            om_ref[0, h * dec_seq:(h + 1) * dec_seq, :] = _dot(
                oln[h * dec_seq:(h + 1) * dec_seq, :], wuv_ref[h])


def _sample_attention(layer, page_table, qsb, ql, qr, ksn, vsn, cn, rn, wuv,
                      cache_k, cache_v, cache_c, cache_r, pages_per_step):
    bsz, rows, _ = qsb.shape
    dec_seq = rows // SB_HEADS
    n_pages = page_table.shape[1]
    P = pages_per_step
    n_steps = n_pages // P

    def per_b(w):
        return pl.BlockSpec((1,) + w, lambda b, c, pt: (b,) + (0,) * len(w))

    def page_spec(width, p):
        return pl.BlockSpec(
            (1, 1, PAGE_SIZE, width),
            lambda b, c, pt: (layer, pt[b, (n_steps - 1 - c) * P + p], 0, 0))

    in_specs = [
        per_b((rows, SLOT)), per_b((rows, KV_RANK)), per_b((rows, ROPE_DIM)),
        per_b((dec_seq, SLOT)), per_b((dec_seq, SLOT)), per_b((dec_seq, KV_RANK)), per_b((dec_seq, ROPE_DIM)),
        pl.BlockSpec((MLA_HEADS, KV_RANK, V_DIM), lambda b, c, pt: (0, 0, 0)),
    ]
    args = [qsb, ql, qr, ksn, vsn, cn, rn, wuv]
    for cache, width in ((cache_k, SLOT), (cache_v, SLOT), (cache_c, KV_RANK), (cache_r, ROPE_DIM)):
        for p in range(P):
            in_specs.append(page_spec(width, p))
            args.append(cache)
    grid_spec = pltpu.PrefetchScalarGridSpec(
        num_scalar_prefetch=1,
        grid=(bsz, n_steps),
        in_specs=in_specs,
        out_specs=[per_b((rows, SLOT)), per_b((rows, V_DIM))],
        scratch_shapes=[
            pltpu.VMEM((rows, SLOT), F32), pltpu.VMEM((rows, 1), F32),
            pltpu.VMEM((rows, 1), F32), pltpu.VMEM((rows, 1), F32),
            pltpu.VMEM((rows, KV_RANK), F32)],
    )
    return pl.pallas_call(
        functools.partial(_dattn_body, P, dec_seq),
        grid_spec=grid_spec,
        out_shape=[jax.ShapeDtypeStruct((bsz, rows, SLOT), F32),
                   jax.ShapeDtypeStruct((bsz, rows, V_DIM), F32)],
        compiler_params=pltpu.CompilerParams(
            dimension_semantics=("arbitrary", "arbitrary"), vmem_limit_bytes=VMEM_LIMIT),
        name="sample_attention",
    )(page_table, *args)


def _post_body(final, ff_chunk, x_ref, osb_ref, om_ref, g1_ref, sh2_ref, sc2_ref, g2_ref,
               gosb_ref, gom_ref, wout_ref, gmlp_ref, w1_ref, w2_ref, fn_ref, out_ref):
    half = osb_ref.shape[-1]
    o1 = _rms(osb_ref[0], gosb_ref[...]).astype(BF16)
    o2 = _rms(om_ref[0], gom_ref[...]).astype(BF16)
    att = _dot(o1, wout_ref[:half, :]) + _dot(o2, wout_ref[half:, :])
    x1 = x_ref[0] + g1_ref[0] * att
    hb = (_rms(x1, gmlp_ref[...]) * (1.0 + sc2_ref[0]) + sh2_ref[0]).astype(BF16)
    d_ff = w1_ref.shape[1]
    y = jnp.zeros_like(x1)
    for c in range(d_ff // ff_chunk):
        hid = _dot(hb, w1_ref[:, c * ff_chunk:(c + 1) * ff_chunk])
        hid = jnp.square(jnp.maximum(hid, 0.0)).astype(BF16)
        y = y + _dot(hid, w2_ref[c * ff_chunk:(c + 1) * ff_chunk, :])
    x2 = x1 + g2_ref[0] * y
    if final:
        x2 = _rms(x2, fn_ref[...])
    out_ref[0] = x2


def _post(x, osb, om, g1, sh2, sc2, g2, wl, final_norm, tile, final):
    bsz, t, d = x.shape
    r = g1.shape[1]
    nt = t // tile
    half = osb.shape[-1]
    d_ff = wl["w1"].shape[1]
    if r == 1:
        mod_spec = pl.BlockSpec((1, 1, d), lambda b, i: (b, 0, 0))
    else:
        mod_spec = pl.BlockSpec((1, tile, d), lambda b, i: (b, i, 0))
    tok = lambda w: pl.BlockSpec((1, tile, w), lambda b, i: (b, i, 0))
    const = lambda shape: pl.BlockSpec(shape, lambda b, i: (0,) * len(shape), pipeline_mode=pl.Buffered(1))
    return pl.pallas_call(
        functools.partial(_post_body, final, 1024),
        grid=(bsz, nt),
        in_specs=[tok(d), tok(half), tok(half), mod_spec, mod_spec, mod_spec, mod_spec,
                  const((1, half)), const((1, half)), const((2 * half, d)), const((1, d)),
                  const((d, d_ff)), const((d_ff, d)), const((1, d))],
        out_specs=tok(d),
        out_shape=jax.ShapeDtypeStruct((bsz, t, d), F32),
        compiler_params=pltpu.CompilerParams(
            dimension_semantics=("arbitrary", "arbitrary"), vmem_limit_bytes=VMEM_LIMIT),
        name="post_final" if final else "post",
    )(x, osb, om, g1, sh2, sc2, g2, wl["gosb"], wl["gom"], wl["wout"], wl["gmlp"],
      wl["w1"], wl["w2"], final_norm)


def _rope_tables(pos):
    half = ROPE_DIM // 2
    inv = ROPE_THETA ** (-jnp.arange(half, dtype=F32) / half)
    ang = pos.astype(F32)[:, None] * inv[None, :]
    cos, sin = jnp.cos(ang), jnp.sin(ang)
    n = pos.shape[0]
    ones = jnp.ones((n, NOPE_DIM), F32)
    z_nope = jnp.zeros((n, NOPE_DIM), F32)
    z_tail = jnp.zeros((n, SLOT - NOPE_DIM - ROPE_DIM), F32)
    c = jnp.concatenate([ones, cos, cos, z_tail], axis=1)
    s = jnp.concatenate([z_nope, -sin, sin, z_tail], axis=1)
    return c * MLA_SCALE, s * MLA_SCALE, c, s


def _swap_halves(w):
    half = ROPE_DIM // 2
    return jnp.concatenate([w[..., half:], w[..., :half]], axis=-1)


def _layer_weights(l, w_in, q_norm, w_uq, kv_norm, w_uk, w_uv, out_norm_sb, out_norm_mla,
                   w_out, mlp_norm, w_mlp1, w_mlp2, attn_norm):
    d = w_in.shape[1]
    wi = w_in[l]
    sb_w = SB_HEADS * HEAD_DIM
    kv_w = SB_KV_HEADS * HEAD_DIM
    wq = wi[:, :sb_w].reshape(d, SB_HEADS, HEAD_DIM) * SB_SCALE
    zq = jnp.zeros_like(wq)
    kv_of_head = jnp.arange(SB_HEADS) // SB_GROUP
    wq_ext = jnp.where((kv_of_head == 0)[None, :, None, None],
                       jnp.stack([wq, zq], axis=2), jnp.stack([zq, wq], axis=2)).reshape(d, SB_Q_EXT)
    o = sb_w
    w_k = wi[:, o:o + kv_w]; o += kv_w
    w_v = wi[:, o:o + kv_w]; o += kv_w
    w_cq = wi[:, o:o + Q_RANK]; o += Q_RANK
    w_ckv = wi[:, o:o + KV_RANK]; o += KV_RANK
    w_kr = wi[:, o:o + ROPE_DIM]
    zn = jnp.zeros((d, NOPE_DIM), F32)
    zt = jnp.zeros((d, SLOT - NOPE_DIM - ROPE_DIM), F32)
    win = jnp.concatenate([wq_ext, w_k, w_v, w_cq, w_ckv,
                           zn, w_kr, zt, zn, _swap_halves(w_kr), zt], axis=1).astype(BF16)
    uq = w_uq[l]
    zq_t = jnp.zeros((Q_RANK, MLA_HEADS, SLOT - NOPE_DIM - ROPE_DIM), F32)
    uq_a = jnp.concatenate([uq, zq_t], axis=2).reshape(Q_RANK, MLA_Q_EXT)
    uq_b = jnp.concatenate([jnp.zeros((Q_RANK, MLA_HEADS, NOPE_DIM), F32),
                            _swap_halves(uq[..., NOPE_DIM:]), zq_t], axis=2).reshape(Q_RANK, MLA_Q_EXT)
    wuq = jnp.concatenate([uq_a, uq_b], axis=1).astype(BF16)
    uk = w_uk[l]
    wk = jnp.concatenate([uk, jnp.zeros((KV_RANK, MLA_HEADS, SLOT - NOPE_DIM), F32)],
                         axis=2).reshape(KV_RANK, MLA_Q_EXT).astype(BF16)
    wuk_abs = jnp.concatenate([jnp.transpose(uk, (1, 2, 0)),
                               jnp.zeros((MLA_HEADS, SLOT - NOPE_DIM, KV_RANK), F32)], axis=1).astype(BF16)
    return {
        "g_attn": attn_norm[l][None, :],
        "win": win, "qn": q_norm[l][None, :], "wuq": wuq, "kvn": kv_norm[l][None, :],
        "wk": wk, "wv": w_uv[l].reshape(KV_RANK, MLA_HEADS * V_DIM).astype(BF16),
        "wuk_abs": wuk_abs, "wuv_h": jnp.transpose(w_uv[l], (1, 0, 2)).astype(BF16),
        "gosb": out_norm_sb[l][None, :], "gom": out_norm_mla[l][None, :],
        "wout": w_out[l].astype(BF16), "gmlp": mlp_norm[l][None, :],
        "w1": w_mlp1[l].astype(BF16), "w2": w_mlp2[l].astype(BF16),
    }


def kernel(x_prompt, x_sample, c_prompt, c_sample, cache_sb_k, cache_sb_v, cache_mla_ckv, cache_mla_krope, page_table, w_ada, b_ada, attn_norm, w_in, q_norm, w_uq, kv_norm, w_uk, w_uv, out_norm_sb, out_norm_mla, w_out, mlp_norm, w_mlp1, w_mlp2, final_norm):
    bsz, seq, d = x_prompt.shape
    dec_b, dec_seq, _ = x_sample.shape
    depth = w_in.shape[0]
    n_pages = page_table.shape[1]
    n_past = n_pages * PAGE_SIZE
    n_tok_s = dec_b * dec_seq
    n_pool = cache_sb_k.shape[1]

    tabs_p = _rope_tables(jnp.arange(seq, dtype=jnp.int32))
    tabs_s = tuple(jnp.tile(t, (dec_b, 1))
                   for t in _rope_tables(n_past + jnp.arange(dec_seq, dtype=jnp.int32)))
    cache_k = cache_sb_k.reshape(depth, n_pool, PAGE_SIZE, SLOT)
    cache_v = cache_sb_v.reshape(depth, n_pool, PAGE_SIZE, SLOT)

    mod = _modulation(jnp.concatenate([c_prompt, c_sample], axis=0), w_ada.astype(BF16), b_ada)
    fnorm = final_norm[None, :]

    xp = x_prompt
    xs = x_sample.reshape(1, n_tok_s, d)
    outs = [[] for _ in range(8)]
    for l in range(depth):
        wl = _layer_weights(l, w_in, q_norm, w_uq, kv_norm, w_uk, w_uv, out_norm_sb, out_norm_mla,
                            w_out, mlp_norm, w_mlp1, w_mlp2, attn_norm)
        final = l == depth - 1

        mp = [m[:, None, :] for m in jnp.split(mod[l, :bsz], 6, axis=-1)]
        (ksb, vsb, ckv, krope, qsb, ksb_b, vsb_b, qm, km, vm) = _proj(
            xp, mp[0], mp[1], wl["g_attn"], wl, tabs_p, min(TOKEN_TILE, seq), absorbed=False)
        osb, om = _prompt_attention(qsb, ksb_b, vsb_b, qm, km, vm, Q_TILE)
        xp = _post(xp, osb, om, mp[2], mp[3], mp[4], mp[5], wl, fnorm, min(TOKEN_TILE, seq), final)
        outs[0].append(ksb.reshape(bsz, seq, SB_KV_HEADS, HEAD_DIM))
        outs[1].append(vsb.reshape(bsz, seq, SB_KV_HEADS, HEAD_DIM))
        outs[2].append(ckv)
        outs[3].append(krope)

        ms = [jnp.repeat(m, dec_seq, axis=0)[None] for m in jnp.split(mod[l, bsz:], 6, axis=-1)]
        (ksb, vsb, ckv, krope, qsb, qm, qlat) = _proj(
            xs, ms[0], ms[1], wl["g_attn"], wl, tabs_s, min(TOKEN_TILE, n_tok_s), absorbed=True)
        to_rows = lambda a, w: jnp.transpose(a.reshape(dec_b, dec_seq, SB_HEADS, w), (0, 2, 1, 3)).reshape(
            dec_b, SB_HEADS * dec_seq, w)
        qsb_r = to_rows(qsb, SLOT)
        ql_r = to_rows(qlat, KV_RANK)
        qr_r = to_rows(qm, SLOT)[:, :, NOPE_DIM:NOPE_DIM + ROPE_DIM]
        osb_raw, om_raw = _sample_attention(
            l, page_table, qsb_r, ql_r, qr_r,
            ksb.reshape(dec_b, dec_seq, SLOT), vsb.reshape(dec_b, dec_seq, SLOT),
            ckv.reshape(dec_b, dec_seq, KV_RANK), krope.reshape(dec_b, dec_seq, ROPE_DIM),
            wl["wuv_h"], cache_k, cache_v, cache_mla_ckv, cache_mla_krope, PAGES_PER_STEP)
        osb_h = osb_raw.reshape(dec_b, SB_KV_HEADS, SB_GROUP, dec_seq, SB_KV_HEADS, HEAD_DIM)
        osb_h = jnp.stack([osb_h[:, n, :, :, n, :] for n in range(SB_KV_HEADS)], axis=1)
        osb_s = jnp.transpose(osb_h.reshape(dec_b, SB_HEADS, dec_seq, HEAD_DIM), (0, 2, 1, 3)).reshape(
            1, n_tok_s, SB_HEADS * HEAD_DIM)
        om_s = jnp.transpose(om_raw.reshape(dec_b, MLA_HEADS, dec_seq, V_DIM), (0, 2, 1, 3)).reshape(
            1, n_tok_s, MLA_HEADS * V_DIM)
        xs = _post(xs, osb_s, om_s, ms[2], ms[3], ms[4], ms[5], wl, fnorm, min(TOKEN_TILE, n_tok_s), final)
        outs[4].append(ksb.reshape(dec_b, dec_seq, SB_KV_HEADS, HEAD_DIM))
        outs[5].append(vsb.reshape(dec_b, dec_seq, SB_KV_HEADS, HEAD_DIM))
        outs[6].append(ckv.reshape(dec_b, dec_seq, KV_RANK))
        outs[7].append(krope.reshape(dec_b, dec_seq, ROPE_DIM))

    return (xp, xs.reshape(dec_b, dec_seq, d)) + tuple(jnp.stack(o) for o in outs)
```

```python
import functools

import jax
import jax.numpy as jnp
from jax import lax
from jax.experimental import pallas as pl
from jax.experimental.pallas import tpu as pltpu

HEAD_DIM = 64
SB_HEADS = 8
SB_KV_HEADS = 2
SB_GROUP = SB_HEADS // SB_KV_HEADS
MLA_HEADS = 8
NOPE_DIM = 64
ROPE_DIM = 32
V_DIM = 64
Q_RANK = 384
KV_RANK = 256
PAGE_SIZE = 128
ROPE_THETA = 10000.0
EPS = 1e-6
LOG2E = 1.4426950408889634
SB_SCALE = HEAD_DIM ** -0.5 * LOG2E
MLA_SCALE = (NOPE_DIM + ROPE_DIM) ** -0.5 * LOG2E
MLA_GROUP = 4
SB_INNER = 8

LANE = 128
SLOT = LANE
SB_Q_EXT = SB_HEADS * SLOT
MLA_Q_EXT = MLA_HEADS * SLOT
OFF_QSB = 0
OFF_KSB = OFF_QSB + SB_Q_EXT
OFF_VSB = OFF_KSB + SB_KV_HEADS * HEAD_DIM
OFF_CQ = OFF_VSB + SB_KV_HEADS * HEAD_DIM
OFF_CKV = OFF_CQ + Q_RANK
OFF_KR = OFF_CKV + KV_RANK
OFF_KRS = OFF_KR + SLOT
IN_EXT = OFF_KRS + SLOT

VMEM_LIMIT = 56 * 1024 * 1024
NEG_BIG = -1e30
TOKEN_TILE = 512
Q_TILE = 256
PAGES_PER_STEP = 16

BF16 = jnp.bfloat16
F32 = jnp.float32


def _rms(x, g):
    return x * lax.rsqrt(jnp.mean(x * x, axis=-1, keepdims=True) + EPS) * g


def _dot(a, b):
    return jnp.dot(a, b, preferred_element_type=F32)


def _dot_nt(a, b):
    return lax.dot_general(a, b, (((1,), (1,)), ((), ())), preferred_element_type=F32)


def _mod_body(c_ref, w_ref, b_ref, out_ref):
    c = c_ref[...]
    s = c * jax.nn.sigmoid(c)
    out_ref[0] = _dot(s.astype(BF16), w_ref[0]) + b_ref[0]


def _modulation(c_all, w_ada_bf, b_ada):
    depth, d, n6 = w_ada_bf.shape
    rows = c_all.shape[0]
    tn = 1536
    return pl.pallas_call(
        _mod_body,
        grid=(depth, n6 // tn),
        in_specs=[
            pl.BlockSpec((rows, d), lambda l, j: (0, 0)),
            pl.BlockSpec((1, d, tn), lambda l, j: (l, 0, j)),
            pl.BlockSpec((1, 1, tn), lambda l, j: (l, 0, j)),
        ],
        out_specs=pl.BlockSpec((1, rows, tn), lambda l, j: (l, 0, j)),
        out_shape=jax.ShapeDtypeStruct((depth, rows, n6), F32),
        compiler_params=pltpu.CompilerParams(
            dimension_semantics=("arbitrary", "arbitrary"), vmem_limit_bytes=VMEM_LIMIT),
        name="modulation",
    )(c_all, w_ada_bf, b_ada.reshape(depth, 1, n6))


def _proj_common(x_ref, sh_ref, sc_ref, g_ref, win_ref, qn_ref, wuq_ref, kvn_ref,
                 cq_ref, sq_ref, ck_ref, sk_ref, ksb_ref, vsb_ref, ckv_ref, kr_ref):
    x = x_ref[0]
    h = _rms(x, g_ref[...]) * (1.0 + sc_ref[0]) + sh_ref[0]
    proj = _dot(h.astype(BF16), win_ref[...])
    ksb = proj[:, OFF_KSB:OFF_VSB]
    vsb = proj[:, OFF_VSB:OFF_CQ]
    ksb_ref[0] = ksb
    vsb_ref[0] = vsb
    cqn = _rms(proj[:, OFF_CQ:OFF_CKV], qn_ref[...])
    q2 = _dot(cqn.astype(BF16), wuq_ref[...])
    cq_t = jnp.concatenate([cq_ref[...]] * MLA_HEADS, axis=1)
    sq_t = jnp.concatenate([sq_ref[...]] * MLA_HEADS, axis=1)
    qm = q2[:, :MLA_Q_EXT] * cq_t + q2[:, MLA_Q_EXT:] * sq_t
    ckv = _rms(proj[:, OFF_CKV:OFF_KR], kvn_ref[...])
    ckv_ref[0] = ckv
    krp = proj[:, OFF_KR:OFF_KRS] * ck_ref[...] + proj[:, OFF_KRS:IN_EXT] * sk_ref[...]
    kr_ref[0] = krp[:, NOPE_DIM:NOPE_DIM + ROPE_DIM]
    return proj[:, OFF_QSB:OFF_KSB], ksb, vsb, qm, ckv, krp


def _proj_prompt_body(x_ref, sh_ref, sc_ref, g_ref, win_ref, qn_ref, wuq_ref, kvn_ref,
                      cq_ref, sq_ref, ck_ref, sk_ref, wk_ref, wv_ref,
                      ksb_ref, vsb_ref, ckv_ref, kr_ref,
                      qsb_o, ksbb_o, vsbb_o, qm_o, km_o, vm_o):
    qsb, ksb, vsb, qm, ckv, krp = _proj_common(
        x_ref, sh_ref, sc_ref, g_ref, win_ref, qn_ref, wuq_ref, kvn_ref,
        cq_ref, sq_ref, ck_ref, sk_ref, ksb_ref, vsb_ref, ckv_ref, kr_ref)
    ksbb_o[0] = ksb.astype(BF16)
    vsbb_o[0] = vsb.astype(BF16)
    ckv_b = ckv.astype(BF16)
    kn = _dot(ckv_b, wk_ref[...])
    vm = _dot(ckv_b, wv_ref[...])
    for h in range(SB_HEADS):
        qsb_o[0, h] = qsb[:, h * SLOT:(h + 1) * SLOT].astype(BF16)
    for h in range(MLA_HEADS):
        qm_o[0, h] = qm[:, h * SLOT:(h + 1) * SLOT].astype(BF16)
        km_o[0, h] = (kn[:, h * SLOT:(h + 1) * SLOT] + krp).astype(BF16)
    for p in range(MLA_HEADS // 2):
        vm_o[0, p] = vm[:, p * SLOT:(p + 1) * SLOT].astype(BF16)


def _proj_sample_body(x_ref, sh_ref, sc_ref, g_ref, win_ref, qn_ref, wuq_ref, kvn_ref,
                      cq_ref, sq_ref, ck_ref, sk_ref, wuk_ref,
                      ksb_ref, vsb_ref, ckv_ref, kr_ref,
                      qsb_o, qm_o, qlat_o):
    qsb, _, _, qm, _, _ = _proj_common(
        x_ref, sh_ref, sc_ref, g_ref, win_ref, qn_ref, wuq_ref, kvn_ref,
        cq_ref, sq_ref, ck_ref, sk_ref, ksb_ref, vsb_ref, ckv_ref, kr_ref)
    qsb_o[0] = qsb.astype(BF16)
    qm_b = qm.astype(BF16)
    qm_o[0] = qm_b
    for h in range(MLA_HEADS):
        qlat_o[0, :, h * KV_RANK:(h + 1) * KV_RANK] = _dot(
            qm_b[:, h * SLOT:(h + 1) * SLOT], wuk_ref[h]).astype(BF16)


def _full(shape):
    n = len(shape)
    return pl.BlockSpec(shape, lambda b, i: (0,) * n)


def _proj(x, sh, sc, g, wl, tabs, tile, absorbed):
    bsz, t, d = x.shape
    r = sh.shape[1]
    assert t % tile == 0, (t, tile)
    nt = t // tile
    if r == 1:
        mod_spec = pl.BlockSpec((1, 1, d), lambda b, i: (b, 0, 0))
    else:
        mod_spec = pl.BlockSpec((1, tile, d), lambda b, i: (b, i, 0))
    tab_spec = pl.BlockSpec((tile, SLOT), lambda b, i: (i, 0))
    tok = lambda w: pl.BlockSpec((1, tile, w), lambda b, i: (b, i, 0))
    heads = lambda n, w: pl.BlockSpec((1, n, tile, w), lambda b, i: (b, 0, i, 0))
    common_in = [
        tok(d), mod_spec, mod_spec, _full((1, d)),
        _full((d, IN_EXT)), _full((1, Q_RANK)), _full((Q_RANK, 2 * MLA_Q_EXT)), _full((1, KV_RANK)),
        tab_spec, tab_spec, tab_spec, tab_spec,
    ]
    common_args = [x, sh, sc, g, wl["win"], wl["qn"], wl["wuq"], wl["kvn"], *tabs]
    common_out_specs = [tok(SB_KV_HEADS * HEAD_DIM), tok(SB_KV_HEADS * HEAD_DIM), tok(KV_RANK), tok(ROPE_DIM)]
    common_out_shape = [
        jax.ShapeDtypeStruct((bsz, t, SB_KV_HEADS * HEAD_DIM), F32),
        jax.ShapeDtypeStruct((bsz, t, SB_KV_HEADS * HEAD_DIM), F32),
        jax.ShapeDtypeStruct((bsz, t, KV_RANK), F32),
        jax.ShapeDtypeStruct((bsz, t, ROPE_DIM), F32),
    ]
    if not absorbed:
        body = _proj_prompt_body
        in_specs = common_in + [_full((KV_RANK, MLA_Q_EXT)), _full((KV_RANK, MLA_HEADS * V_DIM))]
        args = common_args + [wl["wk"], wl["wv"]]
        out_specs = common_out_specs + [
            heads(SB_HEADS, SLOT), tok(SLOT), tok(SLOT),
            heads(MLA_HEADS, SLOT), heads(MLA_HEADS, SLOT), heads(MLA_HEADS // 2, SLOT)]
        out_shape = common_out_shape + [
            jax.ShapeDtypeStruct((bsz, SB_HEADS, t, SLOT), BF16),
            jax.ShapeDtypeStruct((bsz, t, SLOT), BF16),
            jax.ShapeDtypeStruct((bsz, t, SLOT), BF16),
            jax.ShapeDtypeStruct((bsz, MLA_HEADS, t, SLOT), BF16),
            jax.ShapeDtypeStruct((bsz, MLA_HEADS, t, SLOT), BF16),
            jax.ShapeDtypeStruct((bsz, MLA_HEADS // 2, t, SLOT), BF16),
        ]
        name = "proj_prompt"
    else:
        body = _proj_sample_body
        in_specs = common_in + [pl.BlockSpec((MLA_HEADS, SLOT, KV_RANK), lambda b, i: (0, 0, 0))]
        args = common_args + [wl["wuk_abs"]]
        out_specs = common_out_specs + [tok(SB_Q_EXT), tok(MLA_Q_EXT), tok(MLA_HEADS * KV_RANK)]
        out_shape = common_out_shape + [
            jax.ShapeDtypeStruct((bsz, t, SB_Q_EXT), BF16),
            jax.ShapeDtypeStruct((bsz, t, MLA_Q_EXT), BF16),
            jax.ShapeDtypeStruct((bsz, t, MLA_HEADS * KV_RANK), BF16),
        ]
        name = "proj_sample"
    return pl.pallas_call(
        body,
        grid=(bsz, nt),
        in_specs=in_specs,
        out_specs=out_specs,
        out_shape=out_shape,
        compiler_params=pltpu.CompilerParams(
            dimension_semantics=("arbitrary", "arbitrary"), vmem_limit_bytes=VMEM_LIMIT),
        name=name,
    )(*args)


def _softplus2(z):
    return jnp.maximum(z, 0.0) + jnp.log2(1.0 + jnp.exp2(-jnp.abs(z)))


def _sb_weights(z, sp, within, newer):
    return jnp.exp2(z - sp - within - newer)


def _softmax2_update(s, m, l):
    m_new = jnp.maximum(m, jnp.max(s, axis=1, keepdims=True))
    alpha = jnp.exp2(m - m_new)
    p = jnp.exp2(s - m_new)
    l = alpha * l + jnp.sum(p, axis=1, keepdims=True)
    return m_new, l, alpha, p


def _upper(n):
    return (lax.broadcasted_iota(jnp.int32, (n, n), 0)
            > lax.broadcasted_iota(jnp.int32, (n, n), 1)).astype(BF16)


def _pattn_body(tq, qsb_ref, ksb_ref, vsb_ref, qm_ref, km_ref, vm_ref, osb_ref, om_ref, acc_scr):
    i = pl.program_id(1)
    row = lax.broadcasted_iota(jnp.int32, (tq, tq), 0)
    col = lax.broadcasted_iota(jnp.int32, (tq, tq), 1)
    strict = col < row
    incl = col <= row
    upper = _upper(tq)
    lane = lax.broadcasted_iota(jnp.int32, (tq, SLOT), 1)
    low_half = lane < HEAD_DIM

    def key_rows(j):
        return pl.ds(pl.multiple_of(j * tq, tq), tq)

    strict_all = jnp.concatenate([strict] * SB_INNER, axis=0)
    for first in range(0, SB_HEADS, SB_INNER):
        q = qsb_ref[0, first:first + SB_INNER].reshape(SB_INNER * tq, SLOT)

        def sb_block(j, carry, mask, q=q):
            o, newer = carry
            k = ksb_ref[0, key_rows(j), :]
            v = vsb_ref[0, key_rows(j), :]
            z = _dot_nt(q, k)
            sp = _softplus2(z)
            if mask is not None:
                sp = jnp.where(mask, sp, 0.0)
            within = _dot(sp.astype(BF16), upper)
            a = _sb_weights(z, sp, within, newer)
            if mask is not None:
                a = jnp.where(mask, a, 0.0)
            return (o + _dot(a.astype(BF16), v), newer + jnp.sum(sp, axis=1, keepdims=True))

        c = (jnp.zeros((SB_INNER * tq, SLOT), F32), jnp.zeros((SB_INNER * tq, 1), F32))
        c = sb_block(i, c, strict_all)
        c = lax.fori_loop(0, i, lambda jj, cc, f=sb_block: f(i - 1 - jj, cc, None), c)
        acc_scr[first:first + SB_INNER] = c[0].reshape(SB_INNER, tq, SLOT)

    for c in range(SB_HEADS // 2):
        a, b = acc_scr[2 * c], acc_scr[2 * c + 1]
        if (2 * c) // SB_GROUP == 0:
            b = pltpu.roll(b, HEAD_DIM, axis=1)
        else:
            a = pltpu.roll(a, HEAD_DIM, axis=1)
        osb_ref[0, :, c * SLOT:(c + 1) * SLOT] = jnp.where(low_half, a, b)

    for first in range(0, MLA_HEADS, MLA_GROUP):
        hs = list(range(first, first + MLA_GROUP))
        qs = [qm_ref[0, h] for h in hs]

        def mla_block(j, carry, mask, hs=hs, qs=qs):
            out = []
            for g, h in enumerate(hs):
                m, l, o = carry[g]
                k = km_ref[0, h, key_rows(j), :]
                v = vm_ref[0, h // 2, key_rows(j), :]
                s = _dot_nt(qs[g], k)
                if mask is not None:
                    s = jnp.where(mask, s, NEG_BIG)
                m, l, alpha, p = _softmax2_update(s, m, l)
                out.append((m, l, alpha * o + _dot(p.astype(BF16), v)))
            return tuple(out)

        c = tuple((jnp.full((tq, 1), NEG_BIG, F32), jnp.zeros((tq, 1), F32), jnp.zeros((tq, SLOT), F32))
                  for _ in hs)
        c = mla_block(i, c, incl)
        c = lax.fori_loop(0, i, lambda j, cc, f=mla_block: f(j, cc, None), c)
        for g, h in enumerate(hs):
            acc_scr[h] = c[g][2] / c[g][1]

    for c in range(MLA_HEADS // 2):
        om_ref[0, :, c * SLOT:(c + 1) * SLOT] = jnp.where(low_half, acc_scr[2 * c], acc_scr[2 * c + 1])


def _prompt_attention(qsb, ksb, vsb, qm, km, vm, tq):
    bsz, _, t, _ = qsb.shape
    assert t % tq == 0, (t, tq)
    nq = t // tq
    qspec = pl.BlockSpec((1, SB_HEADS, tq, SLOT), lambda b, i: (b, 0, i, 0))
    kv2 = pl.BlockSpec((1, t, SLOT), lambda b, i: (b, 0, 0))
    kfull = lambda n: pl.BlockSpec((1, n, t, SLOT), lambda b, i: (b, 0, 0, 0))
    ospec = pl.BlockSpec((1, tq, SB_HEADS * HEAD_DIM), lambda b, i: (b, i, 0))
    return pl.pallas_call(
        functools.partial(_pattn_body, tq),
        grid=(bsz, nq),
        in_specs=[qspec, kv2, kv2, qspec, kfull(MLA_HEADS), kfull(MLA_HEADS // 2)],
        out_specs=[ospec, ospec],
        out_shape=[jax.ShapeDtypeStruct((bsz, t, SB_HEADS * HEAD_DIM), F32),
                   jax.ShapeDtypeStruct((bsz, t, MLA_HEADS * V_DIM), F32)],
        scratch_shapes=[pltpu.VMEM((SB_HEADS, tq, SLOT), F32)],
        compiler_params=pltpu.CompilerParams(
            dimension_semantics=("arbitrary", "arbitrary"), vmem_limit_bytes=VMEM_LIMIT),
        name="prompt_attention",
    )(qsb, ksb, vsb, qm, km, vm)


def _dattn_body(pages_per_step, dec_seq, pt_ref, qsb_ref, ql_ref, qr_ref,
                ksn_ref, vsn_ref, cn_ref, rn_ref, wuv_ref, *rest):
    P = pages_per_step
    kp = rest[0:P]
    vp = rest[P:2 * P]
    cp = rest[2 * P:3 * P]
    rp = rest[3 * P:4 * P]
    osb_ref, om_ref, o_scr, newer_scr, m_scr, l_scr, ol_scr = rest[4 * P:]
    step = pl.program_id(1)
    rows = SB_HEADS * dec_seq
    pair = 2 * PAGE_SIZE
    qsb = qsb_ref[0]
    ql = ql_ref[0]
    qr = qr_ref[0]

    @pl.when(step == 0)
    def _():
        def padded(ref, width):
            return jnp.concatenate(
                [ref[0], jnp.zeros((PAGE_SIZE - dec_seq, width), F32)], axis=0).astype(BF16)

        k = padded(ksn_ref, SLOT)
        v = padded(vsn_ref, SLOT)
        t_row = lax.rem(lax.broadcasted_iota(jnp.int32, (rows, PAGE_SIZE), 0), dec_seq)
        s_col = lax.broadcasted_iota(jnp.int32, (rows, PAGE_SIZE), 1)
        strict = s_col < t_row
        z = _dot_nt(qsb, k)
        sp = jnp.where(strict, _softplus2(z), 0.0)
        within = _dot(sp.astype(BF16), _upper(PAGE_SIZE))
        a = jnp.where(strict, _sb_weights(z, sp, within, 0.0), 0.0)
        o_scr[...] = _dot(a.astype(BF16), v)
        newer_scr[...] = jnp.sum(sp, axis=1, keepdims=True)
        c = padded(cn_ref, KV_RANK)
        r = padded(rn_ref, ROPE_DIM)
        s = jnp.where(s_col <= t_row, _dot_nt(ql, c) + _dot_nt(qr, r), NEG_BIG)
        m = jnp.max(s, axis=1, keepdims=True)
        p = jnp.exp2(s - m)
        m_scr[...] = m
        l_scr[...] = jnp.sum(p, axis=1, keepdims=True)
        ol_scr[...] = _dot(p.astype(BF16), c)

    kt = [kp[p][0, 0].astype(BF16) for p in range(P)]
    vt = [vp[p][0, 0].astype(BF16) for p in range(P)]
    cc = [cp[p][0, 0].astype(BF16) for p in range(P)]
    rt = [rp[p][0, 0].astype(BF16) for p in range(P)]

    z = jnp.concatenate([_dot(qsb, kt[p]) for p in range(P)], axis=1)
    sp = _softplus2(z)
    sp_pairs = [sp[:, j * pair:(j + 1) * pair] for j in range(P // 2)]
    within = _dot(jnp.concatenate(sp_pairs, axis=0).astype(BF16), _upper(pair))
    newer = newer_scr[...]
    newer_of = [None] * (P // 2)
    for j in range(P // 2 - 1, -1, -1):
        newer_of[j] = newer
        newer = newer + jnp.sum(sp_pairs[j], axis=1, keepdims=True)
    newer_scr[...] = newer
    o = o_scr[...]
    for j in range(P // 2):
        a = _sb_weights(z[:, j * pair:(j + 1) * pair], sp_pairs[j],
                        within[j * rows:(j + 1) * rows], newer_of[j]).astype(BF16)
        o = o + _dot_nt(a[:, :PAGE_SIZE], vt[2 * j]) + _dot_nt(a[:, PAGE_SIZE:], vt[2 * j + 1])
    o_scr[...] = o

    s = jnp.concatenate([_dot_nt(ql, cc[p]) + _dot(qr, rt[p]) for p in range(P)], axis=1)
    m, l, alpha, pr = _softmax2_update(s, m_scr[...], l_scr[...])
    pr = pr.astype(BF16)
    ol = alpha * ol_scr[...]
    for p in range(P):
        ol = ol + _dot(pr[:, p * PAGE_SIZE:(p + 1) * PAGE_SIZE], cc[p])
    m_scr[...] = m
    l_scr[...] = l
    ol_scr[...] = ol

    @pl.when(step == pl.num_programs(1) - 1)
    def _():
        osb_ref[0] = o
        oln = (ol / l).astype(BF16)
        for h in range(MLA_HEADS):
            om_ref[0, h * dec_seq:(h + 1) * dec_seq, :] = _dot(
                oln[h * dec_seq:(h + 1) * dec_seq, :], wuv_ref[h])


def _sample_attention(layer, page_table, qsb, ql, qr, ksn, vsn, cn, rn, wuv,
                      cache_k, cache_v, cache_c, cache_r, pages_per_step):
    bsz, rows, _ = qsb.shape
    dec_seq = rows // SB_HEADS
    n_pages = page_table.shape[1]
    P = pages_per_step
    assert n_pages % P == 0 and P % 2 == 0, (n_pages, P)
    n_steps = n_pages // P

    def per_b(w):
        return pl.BlockSpec((1,) + w, lambda b, c, pt: (b,) + (0,) * len(w))

    def page_spec(shape, p):
        return pl.BlockSpec(
            (1, 1) + shape,
            lambda b, c, pt: (layer, pt[b, (n_steps - 1 - c) * P + p], 0, 0))

    in_specs = [
        per_b((rows, SLOT)), per_b((rows, KV_RANK)), per_b((rows, ROPE_DIM)),
        per_b((dec_seq, SLOT)), per_b((dec_seq, SLOT)), per_b((dec_seq, KV_RANK)), per_b((dec_seq, ROPE_DIM)),
        pl.BlockSpec((MLA_HEADS, KV_RANK, V_DIM), lambda b, c, pt: (0, 0, 0)),
    ]
    args = [qsb, ql, qr, ksn, vsn, cn, rn, wuv]
    for cache in (cache_k, cache_v, cache_c, cache_r):
        for p in range(P):
            in_specs.append(page_spec(cache.shape[2:], p))
            args.append(cache)
    grid_spec = pltpu.PrefetchScalarGridSpec(
        num_scalar_prefetch=1,
        grid=(bsz, n_steps),
        in_specs=in_specs,
        out_specs=[per_b((rows, SLOT)), per_b((rows, V_DIM))],
        scratch_shapes=[
            pltpu.VMEM((rows, SLOT), F32), pltpu.VMEM((rows, 1), F32),
            pltpu.VMEM((rows, 1), F32), pltpu.VMEM((rows, 1), F32),
            pltpu.VMEM((rows, KV_RANK), F32)],
    )
    return pl.pallas_call(
        functools.partial(_dattn_body, P, dec_seq),
        grid_spec=grid_spec,
        out_shape=[jax.ShapeDtypeStruct((bsz, rows, SLOT), F32),
                   jax.ShapeDtypeStruct((bsz, rows, V_DIM), F32)],
        compiler_params=pltpu.CompilerParams(
            dimension_semantics=("arbitrary", "arbitrary"), vmem_limit_bytes=VMEM_LIMIT),
        name="sample_attention",
    )(page_table, *args)


def _post_body(final, ff_chunk, x_ref, osb_ref, om_ref, g1_ref, sh2_ref, sc2_ref, g2_ref,
               gosb_ref, gom_ref, wout_ref, gmlp_ref, w1_ref, w2_ref, fn_ref, out_ref):
    half = osb_ref.shape[-1]
    o1 = _rms(osb_ref[0], gosb_ref[...]).astype(BF16)
    o2 = _rms(om_ref[0], gom_ref[...]).astype(BF16)
    att = _dot(o1, wout_ref[:half, :]) + _dot(o2, wout_ref[half:, :])
    x1 = x_ref[0] + g1_ref[0] * att
    hb = (_rms(x1, gmlp_ref[...]) * (1.0 + sc2_ref[0]) + sh2_ref[0]).astype(BF16)
    d_ff = w1_ref.shape[1]
    y = jnp.zeros_like(x1)
    for c in range(d_ff // ff_chunk):
        hid = _dot(hb, w1_ref[:, c * ff_chunk:(c + 1) * ff_chunk])
        hid = jnp.square(jnp.maximum(hid, 0.0)).astype(BF16)
        y = y + _dot(hid, w2_ref[c * ff_chunk:(c + 1) * ff_chunk, :])
    x2 = x1 + g2_ref[0] * y
    if final:
        x2 = _rms(x2, fn_ref[...])
    out_ref[0] = x2


def _post(x, osb, om, g1, sh2, sc2, g2, wl, final_norm, tile, final):
    bsz, t, d = x.shape
    r = g1.shape[1]
    assert t % tile == 0, (t, tile)
    nt = t // tile
    half = osb.shape[-1]
    d_ff = wl["w1"].shape[1]
    if r == 1:
        mod_spec = pl.BlockSpec((1, 1, d), lambda b, i: (b, 0, 0))
    else:
        mod_spec = pl.BlockSpec((1, tile, d), lambda b, i: (b, i, 0))
    tok = lambda w: pl.BlockSpec((1, tile, w), lambda b, i: (b, i, 0))
    const = lambda shape: pl.BlockSpec(shape, lambda b, i: (0,) * len(shape), pipeline_mode=pl.Buffered(1))
    return pl.pallas_call(
        functools.partial(_post_body, final, 1024),
        grid=(bsz, nt),
        in_specs=[tok(d), tok(half), tok(half), mod_spec, mod_spec, mod_spec, mod_spec,
                  const((1, half)), const((1, half)), const((2 * half, d)), const((1, d)),
                  const((d, d_ff)), const((d_ff, d)), const((1, d))],
        out_specs=tok(d),
        out_shape=jax.ShapeDtypeStruct((bsz, t, d), F32),
        compiler_params=pltpu.CompilerParams(
            dimension_semantics=("arbitrary", "arbitrary"), vmem_limit_bytes=VMEM_LIMIT),
        name="post_final" if final else "post",
    )(x, osb, om, g1, sh2, sc2, g2, wl["gosb"], wl["gom"], wl["wout"], wl["gmlp"],
      wl["w1"], wl["w2"], final_norm)


def _rope_tables(pos):
    half = ROPE_DIM // 2
    inv = ROPE_THETA ** (-jnp.arange(half, dtype=F32) / half)
    ang = pos.astype(F32)[:, None] * inv[None, :]
    cos, sin = jnp.cos(ang), jnp.sin(ang)
    n = pos.shape[0]
    ones = jnp.ones((n, NOPE_DIM), F32)
    z_nope = jnp.zeros((n, NOPE_DIM), F32)
    z_tail = jnp.zeros((n, SLOT - NOPE_DIM - ROPE_DIM), F32)
    c = jnp.concatenate([ones, cos, cos, z_tail], axis=1)
    s = jnp.concatenate([z_nope, -sin, sin, z_tail], axis=1)
    return c * MLA_SCALE, s * MLA_SCALE, c, s


def _swap_halves(w):
    half = ROPE_DIM // 2
    return jnp.concatenate([w[..., half:], w[..., :half]], axis=-1)


def _layer_weights(l, w_in, q_norm, w_uq, kv_norm, w_uk, w_uv, out_norm_sb, out_norm_mla,
                   w_out, mlp_norm, w_mlp1, w_mlp2, attn_norm):
    d = w_in.shape[1]
    wi = w_in[l]
    sb_w = SB_HEADS * HEAD_DIM
    kv_w = SB_KV_HEADS * HEAD_DIM
    wq = wi[:, :sb_w].reshape(d, SB_HEADS, HEAD_DIM) * SB_SCALE
    zq = jnp.zeros_like(wq)
    kv_of_head = jnp.arange(SB_HEADS) // SB_GROUP
    wq_ext = jnp.where((kv_of_head == 0)[None, :, None, None],
                       jnp.stack([wq, zq], axis=2), jnp.stack([zq, wq], axis=2)).reshape(d, SB_Q_EXT)
    o = sb_w
    w_k = wi[:, o:o + kv_w]; o += kv_w
    w_v = wi[:, o:o + kv_w]; o += kv_w
    w_cq = wi[:, o:o + Q_RANK]; o += Q_RANK
    w_ckv = wi[:, o:o + KV_RANK]; o += KV_RANK
    w_kr = wi[:, o:o + ROPE_DIM]
    zn = jnp.zeros((d, NOPE_DIM), F32)
    zt = jnp.zeros((d, SLOT - NOPE_DIM - ROPE_DIM), F32)
    win = jnp.concatenate([wq_ext, w_k, w_v, w_cq, w_ckv,
                           zn, w_kr, zt, zn, _swap_halves(w_kr), zt], axis=1).astype(BF16)
    uq = w_uq[l]
    zq_t = jnp.zeros((Q_RANK, MLA_HEADS, SLOT - NOPE_DIM - ROPE_DIM), F32)
    uq_a = jnp.concatenate([uq, zq_t], axis=2).reshape(Q_RANK, MLA_Q_EXT)
    uq_b = jnp.concatenate([jnp.zeros((Q_RANK, MLA_HEADS, NOPE_DIM), F32),
                            _swap_halves(uq[..., NOPE_DIM:]), zq_t], axis=2).reshape(Q_RANK, MLA_Q_EXT)
    wuq = jnp.concatenate([uq_a, uq_b], axis=1).astype(BF16)
    uk = w_uk[l]
    wk = jnp.concatenate([uk, jnp.zeros((KV_RANK, MLA_HEADS, SLOT - NOPE_DIM), F32)],
                         axis=2).reshape(KV_RANK, MLA_Q_EXT).astype(BF16)
    wuk_abs = jnp.concatenate([jnp.transpose(uk, (1, 2, 0)),
                               jnp.zeros((MLA_HEADS, SLOT - NOPE_DIM, KV_RANK), F32)], axis=1).astype(BF16)
    return {
        "g_attn": attn_norm[l][None, :],
        "win": win, "qn": q_norm[l][None, :], "wuq": wuq, "kvn": kv_norm[l][None, :],
        "wk": wk, "wv": w_uv[l].reshape(KV_RANK, MLA_HEADS * V_DIM).astype(BF16),
        "wuk_abs": wuk_abs, "wuv_h": jnp.transpose(w_uv[l], (1, 0, 2)).astype(BF16),
        "gosb": out_norm_sb[l][None, :], "gom": out_norm_mla[l][None, :],
        "wout": w_out[l].astype(BF16), "gmlp": mlp_norm[l][None, :],
        "w1": w_mlp1[l].astype(BF16), "w2": w_mlp2[l].astype(BF16),
    }


def kernel(x_prompt, x_sample, c_prompt, c_sample, cache_sb_k, cache_sb_v, cache_mla_ckv, cache_mla_krope, page_table, w_ada, b_ada, attn_norm, w_in, q_norm, w_uq, kv_norm, w_uk, w_uv, out_norm_sb, out_norm_mla, w_out, mlp_norm, w_mlp1, w_mlp2, final_norm):
    bsz, seq, d = x_prompt.shape
    dec_b, dec_seq, _ = x_sample.shape
    depth = w_in.shape[0]
    n_pages = page_table.shape[1]
    n_past = n_pages * PAGE_SIZE
    n_tok_s = dec_b * dec_seq
    n_pool = cache_sb_k.shape[1]

    tabs_p = _rope_tables(jnp.arange(seq, dtype=jnp.int32))
    tabs_s = tuple(jnp.tile(t, (dec_b, 1))
                   for t in _rope_tables(n_past + jnp.arange(dec_seq, dtype=jnp.int32)))
    cache_k = jnp.transpose(cache_sb_k, (0, 1, 3, 4, 2)).reshape(depth, n_pool, SLOT, PAGE_SIZE)
    cache_v = jnp.transpose(cache_sb_v, (0, 1, 3, 4, 2)).reshape(depth, n_pool, SLOT, PAGE_SIZE)
    cache_r = jnp.transpose(cache_mla_krope, (0, 1, 3, 2))

    mod = _modulation(jnp.concatenate([c_prompt, c_sample], axis=0), w_ada.astype(BF16), b_ada)
    fnorm = final_norm[None, :]

    xp = x_prompt
    xs = x_sample.reshape(1, n_tok_s, d)
    outs = [[] for _ in range(8)]
    for l in range(depth):
        wl = _layer_weights(l, w_in, q_norm, w_uq, kv_norm, w_uk, w_uv, out_norm_sb, out_norm_mla,
                            w_out, mlp_norm, w_mlp1, w_mlp2, attn_norm)
        final = l == depth - 1

        mp = [m[:, None, :] for m in jnp.split(mod[l, :bsz], 6, axis=-1)]
        (ksb, vsb, ckv, krope, qsb, ksb_b, vsb_b, qm, km, vm) = _proj(
            xp, mp[0], mp[1], wl["g_attn"], wl, tabs_p, min(TOKEN_TILE, seq), absorbed=False)
        osb, om = _prompt_attention(qsb, ksb_b, vsb_b, qm, km, vm, Q_TILE)
        xp = _post(xp, osb, om, mp[2], mp[3], mp[4], mp[5], wl, fnorm, min(TOKEN_TILE, seq), final)
        outs[0].append(ksb.reshape(bsz, seq, SB_KV_HEADS, HEAD_DIM))
        outs[1].append(vsb.reshape(bsz, seq, SB_KV_HEADS, HEAD_DIM))
        outs[2].append(ckv)
        outs[3].append(krope)

        ms = [jnp.repeat(m, dec_seq, axis=0)[None] for m in jnp.split(mod[l, bsz:], 6, axis=-1)]
        (ksb, vsb, ckv, krope, qsb, qm, qlat) = _proj(
            xs, ms[0], ms[1], wl["g_attn"], wl, tabs_s, min(TOKEN_TILE, n_tok_s), absorbed=True)
        to_rows = lambda a, w: jnp.transpose(a.reshape(dec_b, dec_seq, SB_HEADS, w), (0, 2, 1, 3)).reshape(
            dec_b, SB_HEADS * dec_seq, w)
        qsb_r = to_rows(qsb, SLOT)
        ql_r = to_rows(qlat, KV_RANK)
        qr_r = to_rows(qm, SLOT)[:, :, NOPE_DIM:NOPE_DIM + ROPE_DIM]
        osb_raw, om_raw = _sample_attention(
            l, page_table, qsb_r, ql_r, qr_r,
            ksb.reshape(dec_b, dec_seq, SLOT), vsb.reshape(dec_b, dec_seq, SLOT),
            ckv.reshape(dec_b, dec_seq, KV_RANK), krope.reshape(dec_b, dec_seq, ROPE_DIM),
            wl["wuv_h"], cache_k, cache_v, cache_mla_ckv, cache_r, PAGES_PER_STEP)
        osb_h = osb_raw.reshape(dec_b, SB_KV_HEADS, SB_GROUP, dec_seq, SB_KV_HEADS, HEAD_DIM)
        osb_h = jnp.stack([osb_h[:, n, :, :, n, :] for n in range(SB_KV_HEADS)], axis=1)
        osb_s = jnp.transpose(osb_h.reshape(dec_b, SB_HEADS, dec_seq, HEAD_DIM), (0, 2, 1, 3)).reshape(
            1, n_tok_s, SB_HEADS * HEAD_DIM)
        om_s = jnp.transpose(om_raw.reshape(dec_b, MLA_HEADS, dec_seq, V_DIM), (0, 2, 1, 3)).reshape(
            1, n_tok_s, MLA_HEADS * V_DIM)
        xs = _post(xs, osb_s, om_s, ms[2], ms[3], ms[4], ms[5], wl, fnorm, min(TOKEN_TILE, n_tok_s), final)
        outs[4].append(ksb.reshape(dec_b, dec_seq, SB_KV_HEADS, HEAD_DIM))
        outs[5].append(vsb.reshape(dec_b, dec_seq, SB_KV_HEADS, HEAD_DIM))
        outs[6].append(ckv.reshape(dec_b, dec_seq, KV_RANK))
        outs[7].append(krope.reshape(dec_b, dec_seq, ROPE_DIM))

    return (xp, xs.reshape(dec_b, dec_seq, d)) + tuple(jnp.stack(o) for o in outs)
```

```python
import functools

import jax
import jax.numpy as jnp
from jax import lax
from jax.experimental import pallas as pl
from jax.experimental.pallas import tpu as pltpu

HEAD_DIM = 64
SB_HEADS = 8
SB_KV_HEADS = 2
SB_GROUP = SB_HEADS // SB_KV_HEADS
MLA_HEADS = 8
NOPE_DIM = 64
ROPE_DIM = 32
V_DIM = 64
Q_RANK = 384
KV_RANK = 256
PAGE_SIZE = 128
ROPE_THETA = 10000.0
EPS = 1e-6
LOG2E = 1.4426950408889634
SB_SCALE = HEAD_DIM ** -0.5 * LOG2E
MLA_SCALE = (NOPE_DIM + ROPE_DIM) ** -0.5 * LOG2E
MLA_GROUP = 8
SB_INNER = 8

LANE = 128
SLOT = LANE
SB_Q_EXT = SB_HEADS * SLOT
MLA_Q_EXT = MLA_HEADS * SLOT
OFF_QSB = 0
OFF_KSB = OFF_QSB + SB_Q_EXT
OFF_VSB = OFF_KSB + SB_KV_HEADS * HEAD_DIM
OFF_CQ = OFF_VSB + SB_KV_HEADS * HEAD_DIM
OFF_CKV = OFF_CQ + Q_RANK
OFF_KR = OFF_CKV + KV_RANK
OFF_KRS = OFF_KR + SLOT
IN_EXT = OFF_KRS + SLOT

VMEM_LIMIT = 56 * 1024 * 1024
NEG_BIG = -1e30
TOKEN_TILE = 512
Q_TILE = 256
PAGES_PER_STEP = 16

BF16 = jnp.bfloat16
F32 = jnp.float32


def _rms(x, g):
    return x * lax.rsqrt(jnp.mean(x * x, axis=-1, keepdims=True) + EPS) * g


def _dot(a, b):
    return jnp.dot(a, b, preferred_element_type=F32)


def _dot_nt(a, b):
    return lax.dot_general(a, b, (((1,), (1,)), ((), ())), preferred_element_type=F32)


def _mod_body(c_ref, w_ref, b_ref, out_ref):
    c = c_ref[...]
    s = c * jax.nn.sigmoid(c)
    out_ref[0] = _dot(s.astype(BF16), w_ref[0]) + b_ref[0]


def _modulation(c_all, w_ada_bf, b_ada):
    depth, d, n6 = w_ada_bf.shape
    rows = c_all.shape[0]
    tn = 1536
    return pl.pallas_call(
        _mod_body,
        grid=(depth, n6 // tn),
        in_specs=[
            pl.BlockSpec((rows, d), lambda l, j: (0, 0)),
            pl.BlockSpec((1, d, tn), lambda l, j: (l, 0, j)),
            pl.BlockSpec((1, 1, tn), lambda l, j: (l, 0, j)),
        ],
        out_specs=pl.BlockSpec((1, rows, tn), lambda l, j: (l, 0, j)),
        out_shape=jax.ShapeDtypeStruct((depth, rows, n6), F32),
        compiler_params=pltpu.CompilerParams(
            dimension_semantics=("arbitrary", "arbitrary"), vmem_limit_bytes=VMEM_LIMIT),
        name="modulation",
    )(c_all, w_ada_bf, b_ada.reshape(depth, 1, n6))


def _proj_common(x_ref, sh_ref, sc_ref, g_ref, win_ref, qn_ref, wuq_ref, kvn_ref,
                 cq_ref, sq_ref, ck_ref, sk_ref, ksb_ref, vsb_ref, ckv_ref, kr_ref):
    x = x_ref[0]
    h = _rms(x, g_ref[...]) * (1.0 + sc_ref[0]) + sh_ref[0]
    proj = _dot(h.astype(BF16), win_ref[...])
    ksb = proj[:, OFF_KSB:OFF_VSB]
    vsb = proj[:, OFF_VSB:OFF_CQ]
    ksb_ref[0] = ksb
    vsb_ref[0] = vsb
    cqn = _rms(proj[:, OFF_CQ:OFF_CKV], qn_ref[...])
    q2 = _dot(cqn.astype(BF16), wuq_ref[...])
    cq_t = jnp.concatenate([cq_ref[...]] * MLA_HEADS, axis=1)
    sq_t = jnp.concatenate([sq_ref[...]] * MLA_HEADS, axis=1)
    qm = q2[:, :MLA_Q_EXT] * cq_t + q2[:, MLA_Q_EXT:] * sq_t
    ckv = _rms(proj[:, OFF_CKV:OFF_KR], kvn_ref[...])
    ckv_ref[0] = ckv
    krp = proj[:, OFF_KR:OFF_KRS] * ck_ref[...] + proj[:, OFF_KRS:IN_EXT] * sk_ref[...]
    kr_ref[0] = krp[:, NOPE_DIM:NOPE_DIM + ROPE_DIM]
    return proj[:, OFF_QSB:OFF_KSB], ksb, vsb, qm, ckv, krp


def _proj_prompt_body(x_ref, sh_ref, sc_ref, g_ref, win_ref, qn_ref, wuq_ref, kvn_ref,
                      cq_ref, sq_ref, ck_ref, sk_ref, wk_ref, wv_ref,
                      ksb_ref, vsb_ref, ckv_ref, kr_ref,
                      qsb_o, ksbb_o, vsbb_o, qm_o, km_o, vm_o):
    qsb, ksb, vsb, qm, ckv, krp = _proj_common(
        x_ref, sh_ref, sc_ref, g_ref, win_ref, qn_ref, wuq_ref, kvn_ref,
        cq_ref, sq_ref, ck_ref, sk_ref, ksb_ref, vsb_ref, ckv_ref, kr_ref)
    ksbb_o[0] = ksb.astype(BF16)
    vsbb_o[0] = vsb.astype(BF16)
    ckv_b = ckv.astype(BF16)
    kn = _dot(ckv_b, wk_ref[...])
    vm = _dot(ckv_b, wv_ref[...])
    for h in range(SB_HEADS):
        qsb_o[0, h] = qsb[:, h * SLOT:(h + 1) * SLOT].astype(BF16)
    for h in range(MLA_HEADS):
        qm_o[0, h] = qm[:, h * SLOT:(h + 1) * SLOT].astype(BF16)
        km_o[0, h] = (kn[:, h * SLOT:(h + 1) * SLOT] + krp).astype(BF16)
    for p in range(MLA_HEADS // 2):
        vm_o[0, p] = vm[:, p * SLOT:(p + 1) * SLOT].astype(BF16)


def _proj_sample_body(x_ref, sh_ref, sc_ref, g_ref, win_ref, qn_ref, wuq_ref, kvn_ref,
                      cq_ref, sq_ref, ck_ref, sk_ref, wuk_ref,
                      ksb_ref, vsb_ref, ckv_ref, kr_ref,
                      qsb_o, qm_o, qlat_o):
    qsb, _, _, qm, _, _ = _proj_common(
        x_ref, sh_ref, sc_ref, g_ref, win_ref, qn_ref, wuq_ref, kvn_ref,
        cq_ref, sq_ref, ck_ref, sk_ref, ksb_ref, vsb_ref, ckv_ref, kr_ref)
    qsb_o[0] = qsb.astype(BF16)
    qm_b = qm.astype(BF16)
    qm_o[0] = qm_b
    for h in range(MLA_HEADS):
        qlat_o[0, :, h * KV_RANK:(h + 1) * KV_RANK] = _dot(
            qm_b[:, h * SLOT:(h + 1) * SLOT], wuk_ref[h]).astype(BF16)


def _full(shape):
    n = len(shape)
    return pl.BlockSpec(shape, lambda b, i: (0,) * n)


def _proj(x, sh, sc, g, wl, tabs, tile, absorbed):
    bsz, t, d = x.shape
    r = sh.shape[1]
    assert t % tile == 0, (t, tile)
    nt = t // tile
    if r == 1:
        mod_spec = pl.BlockSpec((1, 1, d), lambda b, i: (b, 0, 0))
    else:
        mod_spec = pl.BlockSpec((1, tile, d), lambda b, i: (b, i, 0))
    tab_spec = pl.BlockSpec((tile, SLOT), lambda b, i: (i, 0))
    tok = lambda w: pl.BlockSpec((1, tile, w), lambda b, i: (b, i, 0))
    heads = lambda n, w: pl.BlockSpec((1, n, tile, w), lambda b, i: (b, 0, i, 0))
    common_in = [
        tok(d), mod_spec, mod_spec, _full((1, d)),
        _full((d, IN_EXT)), _full((1, Q_RANK)), _full((Q_RANK, 2 * MLA_Q_EXT)), _full((1, KV_RANK)),
        tab_spec, tab_spec, tab_spec, tab_spec,
    ]
    common_args = [x, sh, sc, g, wl["win"], wl["qn"], wl["wuq"], wl["kvn"], *tabs]
    common_out_specs = [tok(SB_KV_HEADS * HEAD_DIM), tok(SB_KV_HEADS * HEAD_DIM), tok(KV_RANK), tok(ROPE_DIM)]
    common_out_shape = [
        jax.ShapeDtypeStruct((bsz, t, SB_KV_HEADS * HEAD_DIM), F32),
        jax.ShapeDtypeStruct((bsz, t, SB_KV_HEADS * HEAD_DIM), F32),
        jax.ShapeDtypeStruct((bsz, t, KV_RANK), F32),
        jax.ShapeDtypeStruct((bsz, t, ROPE_DIM), F32),
    ]
    if not absorbed:
        body = _proj_prompt_body
        in_specs = common_in + [_full((KV_RANK, MLA_Q_EXT)), _full((KV_RANK, MLA_HEADS * V_DIM))]
        args = common_args + [wl["wk"], wl["wv"]]
        out_specs = common_out_specs + [
            heads(SB_HEADS, SLOT), tok(SLOT), tok(SLOT),
            heads(MLA_HEADS, SLOT), heads(MLA_HEADS, SLOT), heads(MLA_HEADS // 2, SLOT)]
        out_shape = common_out_shape + [
            jax.ShapeDtypeStruct((bsz, SB_HEADS, t, SLOT), BF16),
            jax.ShapeDtypeStruct((bsz, t, SLOT), BF16),
            jax.ShapeDtypeStruct((bsz, t, SLOT), BF16),
            jax.ShapeDtypeStruct((bsz, MLA_HEADS, t, SLOT), BF16),
            jax.ShapeDtypeStruct((bsz, MLA_HEADS, t, SLOT), BF16),
            jax.ShapeDtypeStruct((bsz, MLA_HEADS // 2, t, SLOT), BF16),
        ]
        name = "proj_prompt"
    else:
        body = _proj_sample_body
        in_specs = common_in + [pl.BlockSpec((MLA_HEADS, SLOT, KV_RANK), lambda b, i: (0, 0, 0))]
        args = common_args + [wl["wuk_abs"]]
        out_specs = common_out_specs + [tok(SB_Q_EXT), tok(MLA_Q_EXT), tok(MLA_HEADS * KV_RANK)]
        out_shape = common_out_shape + [
            jax.ShapeDtypeStruct((bsz, t, SB_Q_EXT), BF16),
            jax.ShapeDtypeStruct((bsz, t, MLA_Q_EXT), BF16),
            jax.ShapeDtypeStruct((bsz, t, MLA_HEADS * KV_RANK), BF16),
        ]
        name = "proj_sample"
    return pl.pallas_call(
        body,
        grid=(bsz, nt),
        in_specs=in_specs,
        out_specs=out_specs,
        out_shape=out_shape,
        compiler_params=pltpu.CompilerParams(
            dimension_semantics=("arbitrary", "arbitrary"), vmem_limit_bytes=VMEM_LIMIT),
        name=name,
    )(*args)


def _softplus2(z):
    sign_bit = jnp.uint32(0x80000000)
    neg_abs = lax.bitcast_convert_type(lax.bitcast_convert_type(z, jnp.uint32) | sign_bit, F32)
    return jnp.maximum(z, 0.0) + jnp.log2(1.0 + jnp.exp2(neg_abs))


def _sb_weights(z, sp, within, newer):
    return jnp.exp2(z - sp - within - newer)


def _softmax2_update(s, m, l):
    m_new = jnp.maximum(m, jnp.max(s, axis=1, keepdims=True))
    alpha = jnp.exp2(m - m_new)
    p = jnp.exp2(s - m_new)
    l = alpha * l + jnp.sum(p, axis=1, keepdims=True)
    return m_new, l, alpha, p


def _upper(n):
    return (lax.broadcasted_iota(jnp.int32, (n, n), 0)
            > lax.broadcasted_iota(jnp.int32, (n, n), 1)).astype(BF16)


def _pattn_body(tq, qsb_ref, ksb_ref, vsb_ref, qm_ref, km_ref, vm_ref, osb_ref, om_ref, acc_scr):
    i = pl.program_id(1)
    row = lax.broadcasted_iota(jnp.int32, (tq, tq), 0)
    col = lax.broadcasted_iota(jnp.int32, (tq, tq), 1)
    strict = col < row
    incl = col <= row
    upper = _upper(tq)
    lane = lax.broadcasted_iota(jnp.int32, (tq, SLOT), 1)
    low_half = lane < HEAD_DIM

    def key_rows(j):
        return pl.ds(pl.multiple_of(j * tq, tq), tq)

    strict_all = jnp.concatenate([strict] * SB_INNER, axis=0)
    for first in range(0, SB_HEADS, SB_INNER):
        q = qsb_ref[0, first:first + SB_INNER].reshape(SB_INNER * tq, SLOT)

        def sb_block(j, carry, mask, q=q):
            o, newer = carry
            k = ksb_ref[0, key_rows(j), :]
            v = vsb_ref[0, key_rows(j), :]
            z = _dot_nt(q, k)
            sp = _softplus2(z)
            if mask is not None:
                sp = jnp.where(mask, sp, 0.0)
            within = _dot(sp.astype(BF16), upper)
            a = _sb_weights(z, sp, within, newer)
            if mask is not None:
                a = jnp.where(mask, a, 0.0)
            return (o + _dot(a.astype(BF16), v), newer + jnp.sum(sp, axis=1, keepdims=True))

        c = (jnp.zeros((SB_INNER * tq, SLOT), F32), jnp.zeros((SB_INNER * tq, 1), F32))
        c = sb_block(i, c, strict_all)
        c = lax.fori_loop(0, i, lambda jj, cc, f=sb_block: f(i - 1 - jj, cc, None), c)
        acc_scr[first:first + SB_INNER] = c[0].reshape(SB_INNER, tq, SLOT)

    for c in range(SB_HEADS // 2):
        a, b = acc_scr[2 * c], acc_scr[2 * c + 1]
        if (2 * c) // SB_GROUP == 0:
            b = pltpu.roll(b, HEAD_DIM, axis=1)
        else:
            a = pltpu.roll(a, HEAD_DIM, axis=1)
        osb_ref[0, :, c * SLOT:(c + 1) * SLOT] = jnp.where(low_half, a, b)

    for first in range(0, MLA_HEADS, MLA_GROUP):
        hs = list(range(first, first + MLA_GROUP))
        qs = [qm_ref[0, h] for h in hs]

        def mla_block(j, carry, mask, hs=hs, qs=qs):
            out = []
            for g, h in enumerate(hs):
                m, l, o = carry[g]
                k = km_ref[0, h, key_rows(j), :]
                v = vm_ref[0, h // 2, key_rows(j), :]
                s = _dot_nt(qs[g], k)
                if mask is not None:
                    s = jnp.where(mask, s, NEG_BIG)
                m, l, alpha, p = _softmax2_update(s, m, l)
                out.append((m, l, alpha * o + _dot(p.astype(BF16), v)))
            return tuple(out)

        c = tuple((jnp.full((tq, 1), NEG_BIG, F32), jnp.zeros((tq, 1), F32), jnp.zeros((tq, SLOT), F32))
                  for _ in hs)
        c = mla_block(i, c, incl)
        c = lax.fori_loop(0, i, lambda j, cc, f=mla_block: f(j, cc, None), c)
        for g, h in enumerate(hs):
            acc_scr[h] = c[g][2] / c[g][1]

    for c in range(MLA_HEADS // 2):
        om_ref[0, :, c * SLOT:(c + 1) * SLOT] = jnp.where(low_half, acc_scr[2 * c], acc_scr[2 * c + 1])


def _prompt_attention(qsb, ksb, vsb, qm, km, vm, tq):
    bsz, _, t, _ = qsb.shape
    assert t % tq == 0, (t, tq)
    nq = t // tq
    qspec = pl.BlockSpec((1, SB_HEADS, tq, SLOT), lambda b, i: (b, 0, i, 0))
    kv2 = pl.BlockSpec((1, t, SLOT), lambda b, i: (b, 0, 0))
    kfull = lambda n: pl.BlockSpec((1, n, t, SLOT), lambda b, i: (b, 0, 0, 0))
    ospec = pl.BlockSpec((1, tq, SB_HEADS * HEAD_DIM), lambda b, i: (b, i, 0))
    return pl.pallas_call(
        functools.partial(_pattn_body, tq),
        grid=(bsz, nq),
        in_specs=[qspec, kv2, kv2, qspec, kfull(MLA_HEADS), kfull(MLA_HEADS // 2)],
        out_specs=[ospec, ospec],
        out_shape=[jax.ShapeDtypeStruct((bsz, t, SB_HEADS * HEAD_DIM), F32),
                   jax.ShapeDtypeStruct((bsz, t, MLA_HEADS * V_DIM), F32)],
        scratch_shapes=[pltpu.VMEM((SB_HEADS, tq, SLOT), F32)],
        compiler_params=pltpu.CompilerParams(
            dimension_semantics=("arbitrary", "arbitrary"), vmem_limit_bytes=VMEM_LIMIT),
        name="prompt_attention",
    )(qsb, ksb, vsb, qm, km, vm)


def _dattn_body(pages_per_step, dec_seq, layer, pt_ref, qsb_ref, ql_ref, qr_ref,
                ksn_ref, vsn_ref, cn_ref, rn_ref, wuv_ref, ck_hbm, cv_hbm, cc_hbm, cr_hbm,
                osb_ref, om_ref, kbuf, vbuf, cbuf, rbuf, sem,
                o_scr, newer_scr, m_scr, l_scr, ol_scr):
    P = pages_per_step
    b = pl.program_id(0)
    step = pl.program_id(1)
    n_steps = pl.num_programs(1)
    g = b * n_steps + step
    slot = lax.rem(g, 2)
    rows = SB_HEADS * dec_seq
    pair = 2 * PAGE_SIZE

    def page_copies(slot_, b_, step_, page_of=None):
        out = []
        for p in range(P):
            page = 0 if page_of is None else page_of(b_, (n_steps - 1 - step_) * P + p)
            for a, (hbm, buf) in enumerate(((ck_hbm, kbuf), (cv_hbm, vbuf), (cc_hbm, cbuf), (cr_hbm, rbuf))):
                out.append(pltpu.make_async_copy(hbm.at[layer, page], buf.at[slot_, p], sem.at[slot_, a]))
        return out

    table = lambda b_, i_: pt_ref[b_, i_]

    @pl.when(g == 0)
    def _():
        for cp_ in page_copies(0, 0, 0, table):
            cp_.start()

    @pl.when(g + 1 < pl.num_programs(0) * n_steps)
    def _():
        wrap = step + 1 == n_steps
        for cp_ in page_copies(1 - slot, jnp.where(wrap, b + 1, b), jnp.where(wrap, 0, step + 1), table):
            cp_.start()

    for cp_ in page_copies(slot, b, step):
        cp_.wait()

    qsb = qsb_ref[0]
    ql = ql_ref[0]
    qr = qr_ref[0]

    @pl.when(step == 0)
    def _():
        def padded(ref, width):
            return jnp.concatenate(
                [ref[0], jnp.zeros((PAGE_SIZE - dec_seq, width), F32)], axis=0).astype(BF16)

        k = padded(ksn_ref, SLOT)
        v = padded(vsn_ref, SLOT)
        t_row = lax.rem(lax.broadcasted_iota(jnp.int32, (rows, PAGE_SIZE), 0), dec_seq)
        s_col = lax.broadcasted_iota(jnp.int32, (rows, PAGE_SIZE), 1)
        strict = s_col < t_row
        z = _dot_nt(qsb, k)
        sp = jnp.where(strict, _softplus2(z), 0.0)
        within = _dot(sp.astype(BF16), _upper(PAGE_SIZE))
        a = jnp.where(strict, _sb_weights(z, sp, within, 0.0), 0.0)
        o_scr[...] = _dot(a.astype(BF16), v)
        newer_scr[...] = jnp.sum(sp, axis=1, keepdims=True)
        c = padded(cn_ref, KV_RANK)
        r = padded(rn_ref, ROPE_DIM)
        s = jnp.where(s_col <= t_row, _dot_nt(ql, c) + _dot_nt(qr, r), NEG_BIG)
        m = jnp.max(s, axis=1, keepdims=True)
        p = jnp.exp2(s - m)
        m_scr[...] = m
        l_scr[...] = jnp.sum(p, axis=1, keepdims=True)
        ol_scr[...] = _dot(p.astype(BF16), c)

    kt = [kbuf[slot, p].astype(BF16) for p in range(P)]
    vt = [vbuf[slot, p].astype(BF16) for p in range(P)]
    cc = [cbuf[slot, p].astype(BF16) for p in range(P)]
    rt = [rbuf[slot, p].astype(BF16) for p in range(P)]

    z = jnp.concatenate([_dot(qsb, kt[p]) for p in range(P)], axis=1)
    sp = _softplus2(z)
    sp_pairs = [sp[:, j * pair:(j + 1) * pair] for j in range(P // 2)]
    within = _dot(jnp.concatenate(sp_pairs, axis=0).astype(BF16), _upper(pair))
    newer = newer_scr[...]
    newer_of = [None] * (P // 2)
    for j in range(P // 2 - 1, -1, -1):
        newer_of[j] = newer
        newer = newer + jnp.sum(sp_pairs[j], axis=1, keepdims=True)
    newer_scr[...] = newer
    o = o_scr[...]
    for j in range(P // 2):
        a = _sb_weights(z[:, j * pair:(j + 1) * pair], sp_pairs[j],
                        within[j * rows:(j + 1) * rows], newer_of[j]).astype(BF16)
        o = o + _dot_nt(a[:, :PAGE_SIZE], vt[2 * j]) + _dot_nt(a[:, PAGE_SIZE:], vt[2 * j + 1])
    o_scr[...] = o

    s = jnp.concatenate([_dot_nt(ql, cc[p]) + _dot(qr, rt[p]) for p in range(P)], axis=1)
    m, l, alpha, pr = _softmax2_update(s, m_scr[...], l_scr[...])
    pr = pr.astype(BF16)
    ol = alpha * ol_scr[...]
    for p in range(P):
        ol = ol + _dot(pr[:, p * PAGE_SIZE:(p + 1) * PAGE_SIZE], cc[p])
    m_scr[...] = m
    l_scr[...] = l
    ol_scr[...] = ol

    @pl.when(step == pl.num_programs(1) - 1)
    def _():
        osb_ref[0] = o
        oln = (ol / l).astype(BF16)
        for h in range(MLA_HEADS):
            om_ref[0, h * dec_seq:(h + 1) * dec_seq, :] = _dot(
                oln[h * dec_seq:(h + 1) * dec_seq, :], wuv_ref[h])


def _sample_attention(layer, page_table, qsb, ql, qr, ksn, vsn, cn, rn, wuv,
                      cache_k, cache_v, cache_c, cache_r, pages_per_step):
    bsz, rows, _ = qsb.shape
    dec_seq = rows // SB_HEADS
    n_pages = page_table.shape[1]
    P = pages_per_step
    assert n_pages % P == 0 and P % 2 == 0, (n_pages, P)
    n_steps = n_pages // P

    def per_b(w):
        return pl.BlockSpec((1,) + w, lambda b, c, pt: (b,) + (0,) * len(w))

    caches = (cache_k, cache_v, cache_c, cache_r)
    in_specs = [
        per_b((rows, SLOT)), per_b((rows, KV_RANK)), per_b((rows, ROPE_DIM)),
        per_b((dec_seq, SLOT)), per_b((dec_seq, SLOT)), per_b((dec_seq, KV_RANK)), per_b((dec_seq, ROPE_DIM)),
        pl.BlockSpec((MLA_HEADS, KV_RANK, V_DIM), lambda b, c, pt: (0, 0, 0)),
    ] + [pl.BlockSpec(memory_space=pl.ANY)] * len(caches)
    args = [qsb, ql, qr, ksn, vsn, cn, rn, wuv, *caches]
    page_bufs = [pltpu.VMEM((2, P) + cache.shape[2:], cache.dtype) for cache in caches]
    grid_spec = pltpu.PrefetchScalarGridSpec(
        num_scalar_prefetch=1,
        grid=(bsz, n_steps),
        in_specs=in_specs,
        out_specs=[per_b((rows, SLOT)), per_b((rows, V_DIM))],
        scratch_shapes=page_bufs + [
            pltpu.SemaphoreType.DMA((2, len(caches))),
            pltpu.VMEM((rows, SLOT), F32), pltpu.VMEM((rows, 1), F32),
            pltpu.VMEM((rows, 1), F32), pltpu.VMEM((rows, 1), F32),
            pltpu.VMEM((rows, KV_RANK), F32)],
    )
    return pl.pallas_call(
        functools.partial(_dattn_body, P, dec_seq, layer),
        grid_spec=grid_spec,
        out_shape=[jax.ShapeDtypeStruct((bsz, rows, SLOT), F32),
                   jax.ShapeDtypeStruct((bsz, rows, V_DIM), F32)],
        compiler_params=pltpu.CompilerParams(
            dimension_semantics=("arbitrary", "arbitrary"), vmem_limit_bytes=VMEM_LIMIT),
        name="sample_attention",
    )(page_table, *args)


def _post_body(final, ff_chunk, x_ref, osb_ref, om_ref, g1_ref, sh2_ref, sc2_ref, g2_ref,
               gosb_ref, gom_ref, wout_ref, gmlp_ref, w1_ref, w2_ref, fn_ref, out_ref):
    half = osb_ref.shape[-1]
    o1 = _rms(osb_ref[0], gosb_ref[...]).astype(BF16)
    o2 = _rms(om_ref[0], gom_ref[...]).astype(BF16)
    att = _dot(o1, wout_ref[:half, :]) + _dot(o2, wout_ref[half:, :])
    x1 = x_ref[0] + g1_ref[0] * att
    hb = (_rms(x1, gmlp_ref[...]) * (1.0 + sc2_ref[0]) + sh2_ref[0]).astype(BF16)
    d_ff = w1_ref.shape[1]
    y = jnp.zeros_like(x1)
    for c in range(d_ff // ff_chunk):
        hid = _dot(hb, w1_ref[:, c * ff_chunk:(c + 1) * ff_chunk])
        hid = jnp.square(jnp.maximum(hid, 0.0)).astype(BF16)
        y = y + _dot(hid, w2_ref[c * ff_chunk:(c + 1) * ff_chunk, :])
    x2 = x1 + g2_ref[0] * y
    if final:
        x2 = _rms(x2, fn_ref[...])
    out_ref[0] = x2


def _post(x, osb, om, g1, sh2, sc2, g2, wl, final_norm, tile, final):
    bsz, t, d = x.shape
    r = g1.shape[1]
    assert t % tile == 0, (t, tile)
    nt = t // tile
    half = osb.shape[-1]
    d_ff = wl["w1"].shape[1]
    if r == 1:
        mod_spec = pl.BlockSpec((1, 1, d), lambda b, i: (b, 0, 0))
    else:
        mod_spec = pl.BlockSpec((1, tile, d), lambda b, i: (b, i, 0))
    tok = lambda w: pl.BlockSpec((1, tile, w), lambda b, i: (b, i, 0))
    const = lambda shape: pl.BlockSpec(shape, lambda b, i: (0,) * len(shape), pipeline_mode=pl.Buffered(1))
    return pl.pallas_call(
        functools.partial(_post_body, final, 1024),
        grid=(bsz, nt),
        in_specs=[tok(d), tok(half), tok(half), mod_spec, mod_spec, mod_spec, mod_spec,
                  const((1, half)), const((1, half)), const((2 * half, d)), const((1, d)),
                  const((d, d_ff)), const((d_ff, d)), const((1, d))],
        out_specs=tok(d),
        out_shape=jax.ShapeDtypeStruct((bsz, t, d), F32),
        compiler_params=pltpu.CompilerParams(
            dimension_semantics=("arbitrary", "arbitrary"), vmem_limit_bytes=VMEM_LIMIT),
        name="post_final" if final else "post",
    )(x, osb, om, g1, sh2, sc2, g2, wl["gosb"], wl["gom"], wl["wout"], wl["gmlp"],
      wl["w1"], wl["w2"], final_norm)


def _rope_tables(pos):
    half = ROPE_DIM // 2
    inv = ROPE_THETA ** (-jnp.arange(half, dtype=F32) / half)
    ang = pos.astype(F32)[:, None] * inv[None, :]
    cos, sin = jnp.cos(ang), jnp.sin(ang)
    n = pos.shape[0]
    ones = jnp.ones((n, NOPE_DIM), F32)
    z_nope = jnp.zeros((n, NOPE_DIM), F32)
    z_tail = jnp.zeros((n, SLOT - NOPE_DIM - ROPE_DIM), F32)
    c = jnp.concatenate([ones, cos, cos, z_tail], axis=1)
    s = jnp.concatenate([z_nope, -sin, sin, z_tail], axis=1)
    return c * MLA_SCALE, s * MLA_SCALE, c, s


def _swap_halves(w):
    half = ROPE_DIM // 2
    return jnp.concatenate([w[..., half:], w[..., :half]], axis=-1)


def _layer_weights(l, w_in, q_norm, w_uq, kv_norm, w_uk, w_uv, out_norm_sb, out_norm_mla,
                   w_out, mlp_norm, w_mlp1, w_mlp2, attn_norm):
    d = w_in.shape[1]
    wi = w_in[l]
    sb_w = SB_HEADS * HEAD_DIM
    kv_w = SB_KV_HEADS * HEAD_DIM
    wq = wi[:, :sb_w].reshape(d, SB_HEADS, HEAD_DIM) * SB_SCALE
    zq = jnp.zeros_like(wq)
    kv_of_head = jnp.arange(SB_HEADS) // SB_GROUP
    wq_ext = jnp.where((kv_of_head == 0)[None, :, None, None],
                       jnp.stack([wq, zq], axis=2), jnp.stack([zq, wq], axis=2)).reshape(d, SB_Q_EXT)
    o = sb_w
    w_k = wi[:, o:o + kv_w]; o += kv_w
    w_v = wi[:, o:o + kv_w]; o += kv_w
    w_cq = wi[:, o:o + Q_RANK]; o += Q_RANK
    w_ckv = wi[:, o:o + KV_RANK]; o += KV_RANK
    w_kr = wi[:, o:o + ROPE_DIM]
    zn = jnp.zeros((d, NOPE_DIM), F32)
    zt = jnp.zeros((d, SLOT - NOPE_DIM - ROPE_DIM), F32)
    win = jnp.concatenate([wq_ext, w_k, w_v, w_cq, w_ckv,
                           zn, w_kr, zt, zn, _swap_halves(w_kr), zt], axis=1).astype(BF16)
    uq = w_uq[l]
    zq_t = jnp.zeros((Q_RANK, MLA_HEADS, SLOT - NOPE_DIM - ROPE_DIM), F32)
    uq_a = jnp.concatenate([uq, zq_t], axis=2).reshape(Q_RANK, MLA_Q_EXT)
    uq_b = jnp.concatenate([jnp.zeros((Q_RANK, MLA_HEADS, NOPE_DIM), F32),
                            _swap_halves(uq[..., NOPE_DIM:]), zq_t], axis=2).reshape(Q_RANK, MLA_Q_EXT)
    wuq = jnp.concatenate([uq_a, uq_b], axis=1).astype(BF16)
    uk = w_uk[l]
    wk = jnp.concatenate([uk, jnp.zeros((KV_RANK, MLA_HEADS, SLOT - NOPE_DIM), F32)],
                         axis=2).reshape(KV_RANK, MLA_Q_EXT).astype(BF16)
    wuk_abs = jnp.concatenate([jnp.transpose(uk, (1, 2, 0)),
                               jnp.zeros((MLA_HEADS, SLOT - NOPE_DIM, KV_RANK), F32)], axis=1).astype(BF16)
    return {
        "g_attn": attn_norm[l][None, :],
        "win": win, "qn": q_norm[l][None, :], "wuq": wuq, "kvn": kv_norm[l][None, :],
        "wk": wk, "wv": w_uv[l].reshape(KV_RANK, MLA_HEADS * V_DIM).astype(BF16),
        "wuk_abs": wuk_abs, "wuv_h": jnp.transpose(w_uv[l], (1, 0, 2)).astype(BF16),
        "gosb": out_norm_sb[l][None, :], "gom": out_norm_mla[l][None, :],
        "wout": w_out[l].astype(BF16), "gmlp": mlp_norm[l][None, :],
        "w1": w_mlp1[l].astype(BF16), "w2": w_mlp2[l].astype(BF16),
    }


def kernel(x_prompt, x_sample, c_prompt, c_sample, cache_sb_k, cache_sb_v, cache_mla_ckv, cache_mla_krope, page_table, w_ada, b_ada, attn_norm, w_in, q_norm, w_uq, kv_norm, w_uk, w_uv, out_norm_sb, out_norm_mla, w_out, mlp_norm, w_mlp1, w_mlp2, final_norm):
    bsz, seq, d = x_prompt.shape
    dec_b, dec_seq, _ = x_sample.shape
    depth = w_in.shape[0]
    n_pages = page_table.shape[1]
    n_past = n_pages * PAGE_SIZE
    n_tok_s = dec_b * dec_seq
    n_pool = cache_sb_k.shape[1]

    tabs_p = _rope_tables(jnp.arange(seq, dtype=jnp.int32))
    tabs_s = tuple(jnp.tile(t, (dec_b, 1))
                   for t in _rope_tables(n_past + jnp.arange(dec_seq, dtype=jnp.int32)))
    cache_k = jnp.transpose(cache_sb_k, (0, 1, 3, 4, 2)).reshape(depth, n_pool, SLOT, PAGE_SIZE)
    cache_v = jnp.transpose(cache_sb_v, (0, 1, 3, 4, 2)).reshape(depth, n_pool, SLOT, PAGE_SIZE)
    cache_r = jnp.transpose(cache_mla_krope, (0, 1, 3, 2))

    mod = _modulation(jnp.concatenate([c_prompt, c_sample], axis=0), w_ada.astype(BF16), b_ada)
    fnorm = final_norm[None, :]

    xp = x_prompt
    xs = x_sample.reshape(1, n_tok_s, d)
    outs = [[] for _ in range(8)]
    for l in range(depth):
        wl = _layer_weights(l, w_in, q_norm, w_uq, kv_norm, w_uk, w_uv, out_norm_sb, out_norm_mla,
                            w_out, mlp_norm, w_mlp1, w_mlp2, attn_norm)
        final = l == depth - 1

        mp = [m[:, None, :] for m in jnp.split(mod[l, :bsz], 6, axis=-1)]
        (ksb, vsb, ckv, krope, qsb, ksb_b, vsb_b, qm, km, vm) = _proj(
            xp, mp[0], mp[1], wl["g_attn"], wl, tabs_p, min(TOKEN_TILE, seq), absorbed=False)
        osb, om = _prompt_attention(qsb, ksb_b, vsb_b, qm, km, vm, Q_TILE)
        xp = _post(xp, osb, om, mp[2], mp[3], mp[4], mp[5], wl, fnorm, min(TOKEN_TILE, seq), final)
        outs[0].append(ksb.reshape(bsz, seq, SB_KV_HEADS, HEAD_DIM))
        outs[1].append(vsb.reshape(bsz, seq, SB_KV_HEADS, HEAD_DIM))
        outs[2].append(ckv)
        outs[3].append(krope)

        ms = [jnp.repeat(m, dec_seq, axis=0)[None] for m in jnp.split(mod[l, bsz:], 6, axis=-1)]
        (ksb, vsb, ckv, krope, qsb, qm, qlat) = _proj(
            xs, ms[0], ms[1], wl["g_attn"], wl, tabs_s, min(TOKEN_TILE, n_tok_s), absorbed=True)
        to_rows = lambda a, w: jnp.transpose(a.reshape(dec_b, dec_seq, SB_HEADS, w), (0, 2, 1, 3)).reshape(
            dec_b, SB_HEADS * dec_seq, w)
        qsb_r = to_rows(qsb, SLOT)
        ql_r = to_rows(qlat, KV_RANK)
        qr_r = to_rows(qm, SLOT)[:, :, NOPE_DIM:NOPE_DIM + ROPE_DIM]
        osb_raw, om_raw = _sample_attention(
            l, page_table, qsb_r, ql_r, qr_r,
            ksb.reshape(dec_b, dec_seq, SLOT), vsb.reshape(dec_b, dec_seq, SLOT),
            ckv.reshape(dec_b, dec_seq, KV_RANK), krope.reshape(dec_b, dec_seq, ROPE_DIM),
            wl["wuv_h"], cache_k, cache_v, cache_mla_ckv, cache_r, PAGES_PER_STEP)
        osb_h = osb_raw.reshape(dec_b, SB_KV_HEADS, SB_GROUP, dec_seq, SB_KV_HEADS, HEAD_DIM)
        osb_h = jnp.stack([osb_h[:, n, :, :, n, :] for n in range(SB_KV_HEADS)], axis=1)
        osb_s = jnp.transpose(osb_h.reshape(dec_b, SB_HEADS, dec_seq, HEAD_DIM), (0, 2, 1, 3)).reshape(
            1, n_tok_s, SB_HEADS * HEAD_DIM)
        om_s = jnp.transpose(om_raw.reshape(dec_b, MLA_HEADS, dec_seq, V_DIM), (0, 2, 1, 3)).reshape(
            1, n_tok_s, MLA_HEADS * V_DIM)
        xs = _post(xs, osb_s, om_s, ms[2], ms[3], ms[4], ms[5], wl, fnorm, min(TOKEN_TILE, n_tok_s), final)
        outs[4].append(ksb.reshape(dec_b, dec_seq, SB_KV_HEADS, HEAD_DIM))
        outs[5].append(vsb.reshape(dec_b, dec_seq, SB_KV_HEADS, HEAD_DIM))
        outs[6].append(ckv.reshape(dec_b, dec_seq, KV_RANK))
        outs[7].append(krope.reshape(dec_b, dec_seq, ROPE_DIM))

    return (xp, xs.reshape(dec_b, dec_seq, d)) + tuple(jnp.stack(o) for o in outs)
```

```python
import functools

import jax
import jax.numpy as jnp
from jax import lax
from jax.experimental import pallas as pl
from jax.experimental.pallas import tpu as pltpu

HEAD_DIM = 64
SB_HEADS = 8
SB_KV_HEADS = 2
SB_GROUP = SB_HEADS // SB_KV_HEADS
MLA_HEADS = 8
NOPE_DIM = 64
ROPE_DIM = 32
V_DIM = 64
Q_RANK = 384
KV_RANK = 256
PAGE_SIZE = 128
ROPE_THETA = 10000.0
EPS = 1e-6
LOG2E = 1.4426950408889634
SB_SCALE = HEAD_DIM ** -0.5 * LOG2E
MLA_SCALE = (NOPE_DIM + ROPE_DIM) ** -0.5 * LOG2E
MLA_GROUP = 8
SB_INNER = 8

LANE = 128
SLOT = LANE
SB_Q_EXT = SB_HEADS * SLOT
MLA_Q_EXT = MLA_HEADS * SLOT
OFF_QSB = 0
OFF_KSB = OFF_QSB + SB_Q_EXT
OFF_VSB = OFF_KSB + SB_KV_HEADS * HEAD_DIM
OFF_CQ = OFF_VSB + SB_KV_HEADS * HEAD_DIM
OFF_CKV = OFF_CQ + Q_RANK
OFF_KR = OFF_CKV + KV_RANK
OFF_KRS = OFF_KR + SLOT
IN_EXT = OFF_KRS + SLOT

VMEM_LIMIT = 56 * 1024 * 1024
NEG_BIG = -1e30
TOKEN_TILE = 512
Q_TILE = 256
PAGES_PER_STEP = 16

BF16 = jnp.bfloat16
F32 = jnp.float32


def _rms(x, g):
    return x * lax.rsqrt(jnp.mean(x * x, axis=-1, keepdims=True) + EPS) * g


def _dot(a, b):
    return jnp.dot(a, b, preferred_element_type=F32)


def _dot_nt(a, b):
    return lax.dot_general(a, b, (((1,), (1,)), ((), ())), preferred_element_type=F32)


def _mod_body(c_ref, w_ref, b_ref, out_ref):
    c = c_ref[...]
    s = c * jax.nn.sigmoid(c)
    out_ref[0] = _dot(s.astype(BF16), w_ref[0]) + b_ref[0]


def _modulation(c_all, w_ada_bf, b_ada):
    depth, d, n6 = w_ada_bf.shape
    rows = c_all.shape[0]
    tn = 1536
    return pl.pallas_call(
        _mod_body,
        grid=(depth, n6 // tn),
        in_specs=[
            pl.BlockSpec((rows, d), lambda l, j: (0, 0)),
            pl.BlockSpec((1, d, tn), lambda l, j: (l, 0, j)),
            pl.BlockSpec((1, 1, tn), lambda l, j: (l, 0, j)),
        ],
        out_specs=pl.BlockSpec((1, rows, tn), lambda l, j: (l, 0, j)),
        out_shape=jax.ShapeDtypeStruct((depth, rows, n6), F32),
        compiler_params=pltpu.CompilerParams(
            dimension_semantics=("arbitrary", "arbitrary"), vmem_limit_bytes=VMEM_LIMIT),
        name="modulation",
    )(c_all, w_ada_bf, b_ada.reshape(depth, 1, n6))


def _proj_common(x_ref, sh_ref, sc_ref, g_ref, win_ref, qn_ref, wuq_ref, kvn_ref,
                 cq_ref, sq_ref, ck_ref, sk_ref, ksb_ref, vsb_ref, ckv_ref, kr_ref):
    x = x_ref[0]
    h = _rms(x, g_ref[...]) * (1.0 + sc_ref[0]) + sh_ref[0]
    proj = _dot(h.astype(BF16), win_ref[...])
    ksb = proj[:, OFF_KSB:OFF_VSB]
    vsb = proj[:, OFF_VSB:OFF_CQ]
    ksb_ref[0] = ksb
    vsb_ref[0] = vsb
    cqn = _rms(proj[:, OFF_CQ:OFF_CKV], qn_ref[...])
    q2 = _dot(cqn.astype(BF16), wuq_ref[...])
    cq_t = jnp.concatenate([cq_ref[...]] * MLA_HEADS, axis=1)
    sq_t = jnp.concatenate([sq_ref[...]] * MLA_HEADS, axis=1)
    qm = q2[:, :MLA_Q_EXT] * cq_t + q2[:, MLA_Q_EXT:] * sq_t
    ckv = _rms(proj[:, OFF_CKV:OFF_KR], kvn_ref[...])
    ckv_ref[0] = ckv
    krp = proj[:, OFF_KR:OFF_KRS] * ck_ref[...] + proj[:, OFF_KRS:IN_EXT] * sk_ref[...]
    kr_ref[0] = krp[:, NOPE_DIM:NOPE_DIM + ROPE_DIM]
    return proj[:, OFF_QSB:OFF_KSB], ksb, vsb, qm, ckv, krp


def _proj_prompt_body(x_ref, sh_ref, sc_ref, g_ref, win_ref, qn_ref, wuq_ref, kvn_ref,
                      cq_ref, sq_ref, ck_ref, sk_ref, wk_ref, wv_ref,
                      ksb_ref, vsb_ref, ckv_ref, kr_ref,
                      qsb_o, ksbb_o, vsbb_o, qm_o, km_o, vm_o):
    qsb, ksb, vsb, qm, ckv, krp = _proj_common(
        x_ref, sh_ref, sc_ref, g_ref, win_ref, qn_ref, wuq_ref, kvn_ref,
        cq_ref, sq_ref, ck_ref, sk_ref, ksb_ref, vsb_ref, ckv_ref, kr_ref)
    ksbb_o[0] = ksb.astype(BF16)
    vsbb_o[0] = vsb.astype(BF16)
    ckv_b = ckv.astype(BF16)
    kn = _dot(ckv_b, wk_ref[...])
    vm = _dot(ckv_b, wv_ref[...])
    for h in range(SB_HEADS):
        qsb_o[0, h] = qsb[:, h * SLOT:(h + 1) * SLOT].astype(BF16)
    for h in range(MLA_HEADS):
        qm_o[0, h] = qm[:, h * SLOT:(h + 1) * SLOT].astype(BF16)
        km_o[0, h] = (kn[:, h * SLOT:(h + 1) * SLOT] + krp).astype(BF16)
    for p in range(MLA_HEADS // 2):
        vm_o[0, p] = vm[:, p * SLOT:(p + 1) * SLOT].astype(BF16)


def _proj_sample_body(x_ref, sh_ref, sc_ref, g_ref, win_ref, qn_ref, wuq_ref, kvn_ref,
                      cq_ref, sq_ref, ck_ref, sk_ref, wuk_ref,
                      ksb_ref, vsb_ref, ckv_ref, kr_ref,
                      qsb_o, qm_o, qlat_o):
    qsb, _, _, qm, _, _ = _proj_common(
        x_ref, sh_ref, sc_ref, g_ref, win_ref, qn_ref, wuq_ref, kvn_ref,
        cq_ref, sq_ref, ck_ref, sk_ref, ksb_ref, vsb_ref, ckv_ref, kr_ref)
    qsb_o[0] = qsb.astype(BF16)
    qm_b = qm.astype(BF16)
    qm_o[0] = qm_b
    for h in range(MLA_HEADS):
        qlat_o[0, :, h * KV_RANK:(h + 1) * KV_RANK] = _dot(
            qm_b[:, h * SLOT:(h + 1) * SLOT], wuk_ref[h]).astype(BF16)


def _full(shape):
    n = len(shape)
    return pl.BlockSpec(shape, lambda b, i: (0,) * n)


def _proj(x, sh, sc, g, wl, tabs, tile, absorbed):
    bsz, t, d = x.shape
    r = sh.shape[1]
    assert t % tile == 0, (t, tile)
    nt = t // tile
    if r == 1:
        mod_spec = pl.BlockSpec((1, 1, d), lambda b, i: (b, 0, 0))
    else:
        mod_spec = pl.BlockSpec((1, tile, d), lambda b, i: (b, i, 0))
    tab_spec = pl.BlockSpec((tile, SLOT), lambda b, i: (i, 0))
    tok = lambda w: pl.BlockSpec((1, tile, w), lambda b, i: (b, i, 0))
    heads = lambda n, w: pl.BlockSpec((1, n, tile, w), lambda b, i: (b, 0, i, 0))
    common_in = [
        tok(d), mod_spec, mod_spec, _full((1, d)),
        _full((d, IN_EXT)), _full((1, Q_RANK)), _full((Q_RANK, 2 * MLA_Q_EXT)), _full((1, KV_RANK)),
        tab_spec, tab_spec, tab_spec, tab_spec,
    ]
    common_args = [x, sh, sc, g, wl["win"], wl["qn"], wl["wuq"], wl["kvn"], *tabs]
    common_out_specs = [tok(SB_KV_HEADS * HEAD_DIM), tok(SB_KV_HEADS * HEAD_DIM), tok(KV_RANK), tok(ROPE_DIM)]
    common_out_shape = [
        jax.ShapeDtypeStruct((bsz, t, SB_KV_HEADS * HEAD_DIM), F32),
        jax.ShapeDtypeStruct((bsz, t, SB_KV_HEADS * HEAD_DIM), F32),
        jax.ShapeDtypeStruct((bsz, t, KV_RANK), F32),
        jax.ShapeDtypeStruct((bsz, t, ROPE_DIM), F32),
    ]
    if not absorbed:
        body = _proj_prompt_body
        in_specs = common_in + [_full((KV_RANK, MLA_Q_EXT)), _full((KV_RANK, MLA_HEADS * V_DIM))]
        args = common_args + [wl["wk"], wl["wv"]]
        out_specs = common_out_specs + [
            heads(SB_HEADS, SLOT), tok(SLOT), tok(SLOT),
            heads(MLA_HEADS, SLOT), heads(MLA_HEADS, SLOT), heads(MLA_HEADS // 2, SLOT)]
        out_shape = common_out_shape + [
            jax.ShapeDtypeStruct((bsz, SB_HEADS, t, SLOT), BF16),
            jax.ShapeDtypeStruct((bsz, t, SLOT), BF16),
            jax.ShapeDtypeStruct((bsz, t, SLOT), BF16),
            jax.ShapeDtypeStruct((bsz, MLA_HEADS, t, SLOT), BF16),
            jax.ShapeDtypeStruct((bsz, MLA_HEADS, t, SLOT), BF16),
            jax.ShapeDtypeStruct((bsz, MLA_HEADS // 2, t, SLOT), BF16),
        ]
        name = "proj_prompt"
    else:
        body = _proj_sample_body
        in_specs = common_in + [pl.BlockSpec((MLA_HEADS, SLOT, KV_RANK), lambda b, i: (0, 0, 0))]
        args = common_args + [wl["wuk_abs"]]
        out_specs = common_out_specs + [tok(SB_Q_EXT), tok(MLA_Q_EXT), tok(MLA_HEADS * KV_RANK)]
        out_shape = common_out_shape + [
            jax.ShapeDtypeStruct((bsz, t, SB_Q_EXT), BF16),
            jax.ShapeDtypeStruct((bsz, t, MLA_Q_EXT), BF16),
            jax.ShapeDtypeStruct((bsz, t, MLA_HEADS * KV_RANK), BF16),
        ]
        name = "proj_sample"
    return pl.pallas_call(
        body,
        grid=(bsz, nt),
        in_specs=in_specs,
        out_specs=out_specs,
        out_shape=out_shape,
        compiler_params=pltpu.CompilerParams(
            dimension_semantics=("arbitrary", "arbitrary"), vmem_limit_bytes=VMEM_LIMIT),
        name=name,
    )(*args)


def _softplus2(z):
    sign_bit = jnp.uint32(0x80000000)
    neg_abs = lax.bitcast_convert_type(lax.bitcast_convert_type(z, jnp.uint32) | sign_bit, F32)
    return jnp.maximum(z, 0.0) + jnp.log2(1.0 + jnp.exp2(neg_abs))


def _sb_weights(z, sp, within, newer):
    return jnp.exp2(z - sp - within - newer)


def _softmax2_update(s, m, l):
    m_new = jnp.maximum(m, jnp.max(s, axis=1, keepdims=True))
    alpha = jnp.exp2(m - m_new)
    p = jnp.exp2(s - m_new)
    l = alpha * l + jnp.sum(p, axis=1, keepdims=True)
    return m_new, l, alpha, p


def _upper(n):
    return (lax.broadcasted_iota(jnp.int32, (n, n), 0)
            > lax.broadcasted_iota(jnp.int32, (n, n), 1)).astype(BF16)


def _pattn_body(tq, qsb_ref, ksb_ref, vsb_ref, qm_ref, km_ref, vm_ref, osb_ref, om_ref, acc_scr):
    i = pl.program_id(1)
    row = lax.broadcasted_iota(jnp.int32, (tq, tq), 0)
    col = lax.broadcasted_iota(jnp.int32, (tq, tq), 1)
    strict = col < row
    incl = col <= row
    upper = _upper(tq)
    lane = lax.broadcasted_iota(jnp.int32, (tq, SLOT), 1)
    low_half = lane < HEAD_DIM

    def key_rows(j):
        return pl.ds(pl.multiple_of(j * tq, tq), tq)

    strict_all = jnp.concatenate([strict] * SB_INNER, axis=0)
    for first in range(0, SB_HEADS, SB_INNER):
        q = qsb_ref[0, first:first + SB_INNER].reshape(SB_INNER * tq, SLOT)

        def sb_block(j, carry, mask, q=q):
            o, newer = carry
            k = ksb_ref[0, key_rows(j), :]
            v = vsb_ref[0, key_rows(j), :]
            z = _dot_nt(q, k)
            sp = _softplus2(z)
            if mask is not None:
                sp = jnp.where(mask, sp, 0.0)
            within = _dot(sp.astype(BF16), upper)
            a = _sb_weights(z, sp, within, newer)
            if mask is not None:
                a = jnp.where(mask, a, 0.0)
            return (o + _dot(a.astype(BF16), v), newer + jnp.sum(sp, axis=1, keepdims=True))

        c = (jnp.zeros((SB_INNER * tq, SLOT), F32), jnp.zeros((SB_INNER * tq, 1), F32))
        c = sb_block(i, c, strict_all)
        c = lax.fori_loop(0, i, lambda jj, cc, f=sb_block: f(i - 1 - jj, cc, None), c)
        acc_scr[first:first + SB_INNER] = c[0].reshape(SB_INNER, tq, SLOT)

    for c in range(SB_HEADS // 2):
        a, b = acc_scr[2 * c], acc_scr[2 * c + 1]
        if (2 * c) // SB_GROUP == 0:
            b = pltpu.roll(b, HEAD_DIM, axis=1)
        else:
            a = pltpu.roll(a, HEAD_DIM, axis=1)
        osb_ref[0, :, c * SLOT:(c + 1) * SLOT] = jnp.where(low_half, a, b)

    for first in range(0, MLA_HEADS, MLA_GROUP):
        hs = list(range(first, first + MLA_GROUP))
        qs = [qm_ref[0, h] for h in hs]

        def mla_block(j, carry, mask, hs=hs, qs=qs):
            out = []
            for g, h in enumerate(hs):
                m, l, o = carry[g]
                k = km_ref[0, h, key_rows(j), :]
                v = vm_ref[0, h // 2, key_rows(j), :]
                s = _dot_nt(qs[g], k)
                if mask is not None:
                    s = jnp.where(mask, s, NEG_BIG)
                m, l, alpha, p = _softmax2_update(s, m, l)
                out.append((m, l, alpha * o + _dot(p.astype(BF16), v)))
            return tuple(out)

        c = tuple((jnp.full((tq, 1), NEG_BIG, F32), jnp.zeros((tq, 1), F32), jnp.zeros((tq, SLOT), F32))
                  for _ in hs)
        c = mla_block(i, c, incl)
        c = lax.fori_loop(0, i, lambda j, cc, f=mla_block: f(j, cc, None), c)
        for g, h in enumerate(hs):
            acc_scr[h] = c[g][2] / c[g][1]

    for c in range(MLA_HEADS // 2):
        om_ref[0, :, c * SLOT:(c + 1) * SLOT] = jnp.where(low_half, acc_scr[2 * c], acc_scr[2 * c + 1])


def _prompt_attention(qsb, ksb, vsb, qm, km, vm, tq):
    bsz, _, t, _ = qsb.shape
    assert t % tq == 0, (t, tq)
    nq = t // tq
    qspec = pl.BlockSpec((1, SB_HEADS, tq, SLOT), lambda b, i: (b, 0, i, 0))
    kv2 = pl.BlockSpec((1, t, SLOT), lambda b, i: (b, 0, 0))
    kfull = lambda n: pl.BlockSpec((1, n, t, SLOT), lambda b, i: (b, 0, 0, 0))
    ospec = pl.BlockSpec((1, tq, SB_HEADS * HEAD_DIM), lambda b, i: (b, i, 0))
    return pl.pallas_call(
        functools.partial(_pattn_body, tq),
        grid=(bsz, nq),
        in_specs=[qspec, kv2, kv2, qspec, kfull(MLA_HEADS), kfull(MLA_HEADS // 2)],
        out_specs=[ospec, ospec],
        out_shape=[jax.ShapeDtypeStruct((bsz, t, SB_HEADS * HEAD_DIM), F32),
                   jax.ShapeDtypeStruct((bsz, t, MLA_HEADS * V_DIM), F32)],
        scratch_shapes=[pltpu.VMEM((SB_HEADS, tq, SLOT), F32)],
        compiler_params=pltpu.CompilerParams(
            dimension_semantics=("arbitrary", "arbitrary"), vmem_limit_bytes=VMEM_LIMIT),
        name="prompt_attention",
    )(qsb, ksb, vsb, qm, km, vm)


def _dattn_body(pages_per_step, dec_seq, layer, pt_ref, qsb_ref, ql_ref, qr_ref,
                ksn_ref, vsn_ref, cn_ref, rn_ref, wuv_ref, ck_hbm, cv_hbm, cc_hbm, cr_hbm,
                osb_ref, om_ref, kbuf, vbuf, cbuf, rbuf, sem,
                o_scr, newer_scr, m_scr, l_scr, ol_scr):
    P = pages_per_step
    b = pl.program_id(0)
    step = pl.program_id(1)
    n_steps = pl.num_programs(1)
    g = b * n_steps + step
    slot = lax.rem(g, 2)
    rows = SB_HEADS * dec_seq
    pair = 2 * PAGE_SIZE

    def page_copies(slot_, b_, step_, page_of=None):
        out = []
        for p in range(P):
            page = 0 if page_of is None else page_of(b_, (n_steps - 1 - step_) * P + p)
            for a, (hbm, buf) in enumerate(((ck_hbm, kbuf), (cv_hbm, vbuf), (cc_hbm, cbuf), (cr_hbm, rbuf))):
                out.append(pltpu.make_async_copy(hbm.at[layer, page], buf.at[slot_, p], sem.at[slot_, a]))
        return out

    table = lambda b_, i_: pt_ref[b_, i_]

    @pl.when(g == 0)
    def _():
        for cp_ in page_copies(0, 0, 0, table):
            cp_.start()

    @pl.when(g + 1 < pl.num_programs(0) * n_steps)
    def _():
        wrap = step + 1 == n_steps
        nxt = page_copies(1 - slot, jnp.where(wrap, b + 1, b), jnp.where(wrap, 0, step + 1), table)
        for n_, cp_ in enumerate(nxt):
            cp_.start(priority=(n_ + n_ // 4) % 2)

    for cp_ in page_copies(slot, b, step):
        cp_.wait()

    qsb = qsb_ref[0]
    ql = ql_ref[0]
    qr = qr_ref[0]

    @pl.when(step == 0)
    def _():
        def padded(ref, width):
            return jnp.concatenate(
                [ref[0], jnp.zeros((PAGE_SIZE - dec_seq, width), F32)], axis=0).astype(BF16)

        k = padded(ksn_ref, SLOT)
        v = padded(vsn_ref, SLOT)
        t_row = lax.rem(lax.broadcasted_iota(jnp.int32, (rows, PAGE_SIZE), 0), dec_seq)
        s_col = lax.broadcasted_iota(jnp.int32, (rows, PAGE_SIZE), 1)
        strict = s_col < t_row
        z = _dot_nt(qsb, k)
        sp = jnp.where(strict, _softplus2(z), 0.0)
        within = _dot(sp.astype(BF16), _upper(PAGE_SIZE))
        a = jnp.where(strict, _sb_weights(z, sp, within, 0.0), 0.0)
        o_scr[...] = _dot(a.astype(BF16), v)
        newer_scr[...] = jnp.sum(sp, axis=1, keepdims=True)
        c = padded(cn_ref, KV_RANK)
        r = padded(rn_ref, ROPE_DIM)
        s = jnp.where(s_col <= t_row, _dot_nt(ql, c) + _dot_nt(qr, r), NEG_BIG)
        m = jnp.max(s, axis=1, keepdims=True)
        p = jnp.exp2(s - m)
        m_scr[...] = m
        l_scr[...] = jnp.sum(p, axis=1, keepdims=True)
        ol_scr[...] = _dot(p.astype(BF16), c)

    kt = [kbuf[slot, p].astype(BF16) for p in range(P)]
    vt = [vbuf[slot, p].astype(BF16) for p in range(P)]
    cc = [cbuf[slot, p].astype(BF16) for p in range(P)]
    rt = [rbuf[slot, p].astype(BF16) for p in range(P)]

    z = jnp.concatenate([_dot(qsb, kt[p]) for p in range(P)], axis=1)
    sp = _softplus2(z)
    sp_pairs = [sp[:, j * pair:(j + 1) * pair] for j in range(P // 2)]
    within = _dot(jnp.concatenate(sp_pairs, axis=0).astype(BF16), _upper(pair))
    newer = newer_scr[...]
    newer_of = [None] * (P // 2)
    for j in range(P // 2 - 1, -1, -1):
        newer_of[j] = newer
        newer = newer + jnp.sum(sp_pairs[j], axis=1, keepdims=True)
    newer_scr[...] = newer
    o = o_scr[...]
    for j in range(P // 2):
        a = _sb_weights(z[:, j * pair:(j + 1) * pair], sp_pairs[j],
                        within[j * rows:(j + 1) * rows], newer_of[j]).astype(BF16)
        o = o + _dot_nt(a[:, :PAGE_SIZE], vt[2 * j]) + _dot_nt(a[:, PAGE_SIZE:], vt[2 * j + 1])
    o_scr[...] = o

    s = jnp.concatenate([_dot_nt(ql, cc[p]) + _dot(qr, rt[p]) for p in range(P)], axis=1)
    m, l, alpha, pr = _softmax2_update(s, m_scr[...], l_scr[...])
    pr = pr.astype(BF16)
    ol = alpha * ol_scr[...]
    for p in range(P):
        ol = ol + _dot(pr[:, p * PAGE_SIZE:(p + 1) * PAGE_SIZE], cc[p])
    m_scr[...] = m
    l_scr[...] = l
    ol_scr[...] = ol

    @pl.when(step == pl.num_programs(1) - 1)
    def _():
        osb_ref[0] = o
        oln = (ol / l).astype(BF16)
        for h in range(MLA_HEADS):
            om_ref[0, h * dec_seq:(h + 1) * dec_seq, :] = _dot(
                oln[h * dec_seq:(h + 1) * dec_seq, :], wuv_ref[h])


def _sample_attention(layer, page_table, qsb, ql, qr, ksn, vsn, cn, rn, wuv,
                      cache_k, cache_v, cache_c, cache_r, pages_per_step):
    bsz, rows, _ = qsb.shape
    dec_seq = rows // SB_HEADS
    n_pages = page_table.shape[1]
    P = pages_per_step
    assert n_pages % P == 0 and P % 2 == 0, (n_pages, P)
    n_steps = n_pages // P

    def per_b(w):
        return pl.BlockSpec((1,) + w, lambda b, c, pt: (b,) + (0,) * len(w))

    caches = (cache_k, cache_v, cache_c, cache_r)
    in_specs = [
        per_b((rows, SLOT)), per_b((rows, KV_RANK)), per_b((rows, ROPE_DIM)),
        per_b((dec_seq, SLOT)), per_b((dec_seq, SLOT)), per_b((dec_seq, KV_RANK)), per_b((dec_seq, ROPE_DIM)),
        pl.BlockSpec((MLA_HEADS, KV_RANK, V_DIM), lambda b, c, pt: (0, 0, 0)),
    ] + [pl.BlockSpec(memory_space=pl.ANY)] * len(caches)
    args = [qsb, ql, qr, ksn, vsn, cn, rn, wuv, *caches]
    page_bufs = [pltpu.VMEM((2, P) + cache.shape[2:], cache.dtype) for cache in caches]
    grid_spec = pltpu.PrefetchScalarGridSpec(
        num_scalar_prefetch=1,
        grid=(bsz, n_steps),
        in_specs=in_specs,
        out_specs=[per_b((rows, SLOT)), per_b((rows, V_DIM))],
        scratch_shapes=page_bufs + [
            pltpu.SemaphoreType.DMA((2, len(caches))),
            pltpu.VMEM((rows, SLOT), F32), pltpu.VMEM((rows, 1), F32),
            pltpu.VMEM((rows, 1), F32), pltpu.VMEM((rows, 1), F32),
            pltpu.VMEM((rows, KV_RANK), F32)],
    )
    return pl.pallas_call(
        functools.partial(_dattn_body, P, dec_seq, layer),
        grid_spec=grid_spec,
        out_shape=[jax.ShapeDtypeStruct((bsz, rows, SLOT), F32),
                   jax.ShapeDtypeStruct((bsz, rows, V_DIM), F32)],
        compiler_params=pltpu.CompilerParams(
            dimension_semantics=("arbitrary", "arbitrary"), vmem_limit_bytes=VMEM_LIMIT),
        name="sample_attention",
    )(page_table, *args)


def _post_body(final, ff_chunk, x_ref, osb_ref, om_ref, g1_ref, sh2_ref, sc2_ref, g2_ref,
               gosb_ref, gom_ref, wout_ref, gmlp_ref, w1_ref, w2_ref, fn_ref, out_ref):
    half = osb_ref.shape[-1]
    o1 = _rms(osb_ref[0], gosb_ref[...]).astype(BF16)
    o2 = _rms(om_ref[0], gom_ref[...]).astype(BF16)
    att = _dot(o1, wout_ref[:half, :]) + _dot(o2, wout_ref[half:, :])
    x1 = x_ref[0] + g1_ref[0] * att
    hb = (_rms(x1, gmlp_ref[...]) * (1.0 + sc2_ref[0]) + sh2_ref[0]).astype(BF16)
    d_ff = w1_ref.shape[1]
    y = jnp.zeros_like(x1)
    for c in range(d_ff // ff_chunk):
        hid = _dot(hb, w1_ref[:, c * ff_chunk:(c + 1) * ff_chunk])
        hid = jnp.square(jnp.maximum(hid, 0.0)).astype(BF16)
        y = y + _dot(hid, w2_ref[c * ff_chunk:(c + 1) * ff_chunk, :])
    x2 = x1 + g2_ref[0] * y
    if final:
        x2 = _rms(x2, fn_ref[...])
    out_ref[0] = x2


def _post(x, osb, om, g1, sh2, sc2, g2, wl, final_norm, tile, final):
    bsz, t, d = x.shape
    r = g1.shape[1]
    assert t % tile == 0, (t, tile)
    nt = t // tile
    half = osb.shape[-1]
    d_ff = wl["w1"].shape[1]
    if r == 1:
        mod_spec = pl.BlockSpec((1, 1, d), lambda b, i: (b, 0, 0))
    else:
        mod_spec = pl.BlockSpec((1, tile, d), lambda b, i: (b, i, 0))
    tok = lambda w: pl.BlockSpec((1, tile, w), lambda b, i: (b, i, 0))
    const = lambda shape: pl.BlockSpec(shape, lambda b, i: (0,) * len(shape), pipeline_mode=pl.Buffered(1))
    return pl.pallas_call(
        functools.partial(_post_body, final, 1024),
        grid=(bsz, nt),
        in_specs=[tok(d), tok(half), tok(half), mod_spec, mod_spec, mod_spec, mod_spec,
                  const((1, half)), const((1, half)), const((2 * half, d)), const((1, d)),
                  const((d, d_ff)), const((d_ff, d)), const((1, d))],
        out_specs=tok(d),
        out_shape=jax.ShapeDtypeStruct((bsz, t, d), F32),
        compiler_params=pltpu.CompilerParams(
            dimension_semantics=("arbitrary", "arbitrary"), vmem_limit_bytes=VMEM_LIMIT),
        name="post_final" if final else "post",
    )(x, osb, om, g1, sh2, sc2, g2, wl["gosb"], wl["gom"], wl["wout"], wl["gmlp"],
      wl["w1"], wl["w2"], final_norm)


def _rope_tables(pos):
    half = ROPE_DIM // 2
    inv = ROPE_THETA ** (-jnp.arange(half, dtype=F32) / half)
    ang = pos.astype(F32)[:, None] * inv[None, :]
    cos, sin = jnp.cos(ang), jnp.sin(ang)
    n = pos.shape[0]
    ones = jnp.ones((n, NOPE_DIM), F32)
    z_nope = jnp.zeros((n, NOPE_DIM), F32)
    z_tail = jnp.zeros((n, SLOT - NOPE_DIM - ROPE_DIM), F32)
    c = jnp.concatenate([ones, cos, cos, z_tail], axis=1)
    s = jnp.concatenate([z_nope, -sin, sin, z_tail], axis=1)
    return c * MLA_SCALE, s * MLA_SCALE, c, s


def _swap_halves(w):
    half = ROPE_DIM // 2
    return jnp.concatenate([w[..., half:], w[..., :half]], axis=-1)


def _layer_weights(l, w_in, q_norm, w_uq, kv_norm, w_uk, w_uv, out_norm_sb, out_norm_mla,
                   w_out, mlp_norm, w_mlp1, w_mlp2, attn_norm):
    d = w_in.shape[1]
    wi = w_in[l]
    sb_w = SB_HEADS * HEAD_DIM
    kv_w = SB_KV_HEADS * HEAD_DIM
    wq = wi[:, :sb_w].reshape(d, SB_HEADS, HEAD_DIM) * SB_SCALE
    zq = jnp.zeros_like(wq)
    kv_of_head = jnp.arange(SB_HEADS) // SB_GROUP
    wq_ext = jnp.where((kv_of_head == 0)[None, :, None, None],
                       jnp.stack([wq, zq], axis=2), jnp.stack([zq, wq], axis=2)).reshape(d, SB_Q_EXT)
    o = sb_w
    w_k = wi[:, o:o + kv_w]; o += kv_w
    w_v = wi[:, o:o + kv_w]; o += kv_w
    w_cq = wi[:, o:o + Q_RANK]; o += Q_RANK
    w_ckv = wi[:, o:o + KV_RANK]; o += KV_RANK
    w_kr = wi[:, o:o + ROPE_DIM]
    zn = jnp.zeros((d, NOPE_DIM), F32)
    zt = jnp.zeros((d, SLOT - NOPE_DIM - ROPE_DIM), F32)
    win = jnp.concatenate([wq_ext, w_k, w_v, w_cq, w_ckv,
                           zn, w_kr, zt, zn, _swap_halves(w_kr), zt], axis=1).astype(BF16)
    uq = w_uq[l]
    zq_t = jnp.zeros((Q_RANK, MLA_HEADS, SLOT - NOPE_DIM - ROPE_DIM), F32)
    uq_a = jnp.concatenate([uq, zq_t], axis=2).reshape(Q_RANK, MLA_Q_EXT)
    uq_b = jnp.concatenate([jnp.zeros((Q_RANK, MLA_HEADS, NOPE_DIM), F32),
                            _swap_halves(uq[..., NOPE_DIM:]), zq_t], axis=2).reshape(Q_RANK, MLA_Q_EXT)
    wuq = jnp.concatenate([uq_a, uq_b], axis=1).astype(BF16)
    uk = w_uk[l]
    wk = jnp.concatenate([uk, jnp.zeros((KV_RANK, MLA_HEADS, SLOT - NOPE_DIM), F32)],
                         axis=2).reshape(KV_RANK, MLA_Q_EXT).astype(BF16)
    wuk_abs = jnp.concatenate([jnp.transpose(uk, (1, 2, 0)),
                               jnp.zeros((MLA_HEADS, SLOT - NOPE_DIM, KV_RANK), F32)], axis=1).astype(BF16)
    return {
        "g_attn": attn_norm[l][None, :],
        "win": win, "qn": q_norm[l][None, :], "wuq": wuq, "kvn": kv_norm[l][None, :],
        "wk": wk, "wv": w_uv[l].reshape(KV_RANK, MLA_HEADS * V_DIM).astype(BF16),
        "wuk_abs": wuk_abs, "wuv_h": jnp.transpose(w_uv[l], (1, 0, 2)).astype(BF16),
        "gosb": out_norm_sb[l][None, :], "gom": out_norm_mla[l][None, :],
        "wout": w_out[l].astype(BF16), "gmlp": mlp_norm[l][None, :],
        "w1": w_mlp1[l].astype(BF16), "w2": w_mlp2[l].astype(BF16),
    }


def kernel(x_prompt, x_sample, c_prompt, c_sample, cache_sb_k, cache_sb_v, cache_mla_ckv, cache_mla_krope, page_table, w_ada, b_ada, attn_norm, w_in, q_norm, w_uq, kv_norm, w_uk, w_uv, out_norm_sb, out_norm_mla, w_out, mlp_norm, w_mlp1, w_mlp2, final_norm):
    bsz, seq, d = x_prompt.shape
    dec_b, dec_seq, _ = x_sample.shape
    depth = w_in.shape[0]
    n_pages = page_table.shape[1]
    n_past = n_pages * PAGE_SIZE
    n_tok_s = dec_b * dec_seq
    n_pool = cache_sb_k.shape[1]

    tabs_p = _rope_tables(jnp.arange(seq, dtype=jnp.int32))
    tabs_s = tuple(jnp.tile(t, (dec_b, 1))
                   for t in _rope_tables(n_past + jnp.arange(dec_seq, dtype=jnp.int32)))
    cache_k = jnp.transpose(cache_sb_k, (0, 1, 3, 4, 2)).reshape(depth, n_pool, SLOT, PAGE_SIZE)
    cache_v = jnp.transpose(cache_sb_v, (0, 1, 3, 4, 2)).reshape(depth, n_pool, SLOT, PAGE_SIZE)
    cache_r = jnp.transpose(cache_mla_krope, (0, 1, 3, 2))

    mod = _modulation(jnp.concatenate([c_prompt, c_sample], axis=0), w_ada.astype(BF16), b_ada)
    fnorm = final_norm[None, :]

    xp = x_prompt
    xs = x_sample.reshape(1, n_tok_s, d)
    outs = [[] for _ in range(8)]
    for l in range(depth):
        wl = _layer_weights(l, w_in, q_norm, w_uq, kv_norm, w_uk, w_uv, out_norm_sb, out_norm_mla,
                            w_out, mlp_norm, w_mlp1, w_mlp2, attn_norm)
        final = l == depth - 1

        mp = [m[:, None, :] for m in jnp.split(mod[l, :bsz], 6, axis=-1)]
        (ksb, vsb, ckv, krope, qsb, ksb_b, vsb_b, qm, km, vm) = _proj(
            xp, mp[0], mp[1], wl["g_attn"], wl, tabs_p, min(TOKEN_TILE, seq), absorbed=False)
        osb, om = _prompt_attention(qsb, ksb_b, vsb_b, qm, km, vm, Q_TILE)
        xp = _post(xp, osb, om, mp[2], mp[3], mp[4], mp[5], wl, fnorm, min(TOKEN_TILE, seq), final)
        outs[0].append(ksb.reshape(bsz, seq, SB_KV_HEADS, HEAD_DIM))
        outs[1].append(vsb.reshape(bsz, seq, SB_KV_HEADS, HEAD_DIM))
        outs[2].append(ckv)
        outs[3].append(krope)

        ms = [jnp.repeat(m, dec_seq, axis=0)[None] for m in jnp.split(mod[l, bsz:], 6, axis=-1)]
        (ksb, vsb, ckv, krope, qsb, qm, qlat) = _proj(
            xs, ms[0], ms[1], wl["g_attn"], wl, tabs_s, min(TOKEN_TILE, n_tok_s), absorbed=True)
        to_rows = lambda a, w: jnp.transpose(a.reshape(dec_b, dec_seq, SB_HEADS, w), (0, 2, 1, 3)).reshape(
            dec_b, SB_HEADS * dec_seq, w)
        qsb_r = to_rows(qsb, SLOT)
        ql_r = to_rows(qlat, KV_RANK)
        qr_r = to_rows(qm, SLOT)[:, :, NOPE_DIM:NOPE_DIM + ROPE_DIM]
        osb_raw, om_raw = _sample_attention(
            l, page_table, qsb_r, ql_r, qr_r,
            ksb.reshape(dec_b, dec_seq, SLOT), vsb.reshape(dec_b, dec_seq, SLOT),
            ckv.reshape(dec_b, dec_seq, KV_RANK), krope.reshape(dec_b, dec_seq, ROPE_DIM),
            wl["wuv_h"], cache_k, cache_v, cache_mla_ckv, cache_r, PAGES_PER_STEP)
        osb_h = osb_raw.reshape(dec_b, SB_KV_HEADS, SB_GROUP, dec_seq, SB_KV_HEADS, HEAD_DIM)
        osb_h = jnp.stack([osb_h[:, n, :, :, n, :] for n in range(SB_KV_HEADS)], axis=1)
        osb_s = jnp.transpose(osb_h.reshape(dec_b, SB_HEADS, dec_seq, HEAD_DIM), (0, 2, 1, 3)).reshape(
            1, n_tok_s, SB_HEADS * HEAD_DIM)
        om_s = jnp.transpose(om_raw.reshape(dec_b, MLA_HEADS, dec_seq, V_DIM), (0, 2, 1, 3)).reshape(
            1, n_tok_s, MLA_HEADS * V_DIM)
        xs = _post(xs, osb_s, om_s, ms[2], ms[3], ms[4], ms[5], wl, fnorm, min(TOKEN_TILE, n_tok_s), final)
        outs[4].append(ksb.reshape(dec_b, dec_seq, SB_KV_HEADS, HEAD_DIM))
        outs[5].append(vsb.reshape(dec_b, dec_seq, SB_KV_HEADS, HEAD_DIM))
        outs[6].append(ckv.reshape(dec_b, dec_seq, KV_RANK))
        outs[7].append(krope.reshape(dec_b, dec_seq, ROPE_DIM))

    return (xp, xs.reshape(dec_b, dec_seq, d)) + tuple(jnp.stack(o) for o in outs)
```
